```python
import jax, jax.numpy as jnp
from jax import lax
import numpy as np

D_MODEL = 4096
BATCH = 4
SEQ = 2048
DEPTH = 1
DEC_BATCH = 32
DEC_SEQ = 4
PAST_LEN = 8192
PAGE_SIZE = 128

N_HEADS = 16
HEAD_DIM = 128
D_ATTN = N_HEADS * HEAD_DIM
D_CONV = D_MODEL // 2
CONV_W = 31
MOBA_BLOCK = 256
MOBA_TOPK = 3
Q_CHUNK = 16
D_FF = 4 * D_MODEL
ROPE_THETA = 10000.0
EPS = 1e-6
D_IN = 3 * D_ATTN + 2 * D_CONV + 2 * D_MODEL

kernel_name = "moba_conformer_gated_hybrid_step"


def rmsnorm(x, g):
    xf = x.astype(jnp.float32)
    y = xf * lax.rsqrt(jnp.mean(xf * xf, axis=-1, keepdims=True) + EPS)
    return (y * g.astype(jnp.float32)).astype(x.dtype)


def layernorm(x, g, b):
    xf = x.astype(jnp.float32)
    mu = jnp.mean(xf, axis=-1, keepdims=True)
    var = jnp.mean(jnp.square(xf - mu), axis=-1, keepdims=True)
    y = (xf - mu) * lax.rsqrt(var + EPS)
    return (y * g.astype(jnp.float32) + b.astype(jnp.float32)).astype(x.dtype)


def rope(x, pos):
    half = HEAD_DIM // 2
    inv_freq = ROPE_THETA ** (-jnp.arange(half, dtype=jnp.float32) / half)
    ang = pos.astype(jnp.float32)[:, None] * inv_freq[None, :]
    cos = jnp.cos(ang)[None, :, None, :]
    sin = jnp.sin(ang)[None, :, None, :]
    x1 = x[..., :half].astype(jnp.float32)
    x2 = x[..., half:].astype(jnp.float32)
    return jnp.concatenate([x1 * cos - x2 * sin, x1 * sin + x2 * cos], axis=-1).astype(x.dtype)


def moba_attention(q, k, v, q_start):
    B, Tq, H, Dh = q.shape
    Tk = k.shape[1]
    nb = -(-Tk // MOBA_BLOCK)
    pad = nb * MOBA_BLOCK - Tk
    k = jnp.pad(k, ((0, 0), (0, pad), (0, 0), (0, 0)))
    v = jnp.pad(v, ((0, 0), (0, pad), (0, 0), (0, 0)))
    kb = k.reshape(B, nb, MOBA_BLOCK, H, Dh).transpose(0, 3, 1, 2, 4)
    vb = v.reshape(B, nb, MOBA_BLOCK, H, Dh).transpose(0, 3, 1, 2, 4)
    kmean = jnp.mean(kb.astype(jnp.float32), axis=3)
    topk = min(MOBA_TOPK, nb)
    qc = min(Q_CHUNK, Tq)
    nc = Tq // qc
    qs = q.reshape(B, nc, qc, H, Dh).transpose(1, 0, 3, 2, 4)
    pos = (q_start + jnp.arange(Tq, dtype=jnp.int32)).reshape(nc, qc)
    scale = HEAD_DIM ** -0.5
    bi = jnp.arange(B)[:, None, None, None]
    hi = jnp.arange(H)[None, :, None, None]
    blk_ids = jnp.arange(nb, dtype=jnp.int32)
    offs = jnp.arange(MOBA_BLOCK, dtype=jnp.int32)

    def one_chunk(args):
        qq, pc = args
        own = pc // MOBA_BLOCK
        gate = jnp.einsum('bhqd,bhnd->bhqn', qq.astype(jnp.float32), kmean)
        fully_past = blk_ids[None, :] < own[:, None]
        gate = jnp.where(fully_past[None, None], gate, -jnp.inf)
        top_s, top_i = lax.top_k(gate, topk)
        top_ok = top_s > -jnp.inf
        own_b = jnp.broadcast_to(own[None, None, :, None], (B, H, qc, 1)).astype(top_i.dtype)
        idx = jnp.concatenate([top_i, own_b], axis=-1)
        ok = jnp.concatenate([top_ok, jnp.ones((B, H, qc, 1), bool)], axis=-1)
        kg = kb[bi, hi, idx]
        vg = vb[bi, hi, idx]
        kpos = idx[..., None] * MOBA_BLOCK + offs
        mask = ok[..., None] & (kpos <= pc[None, None, :, None, None])
        s = jnp.einsum('bhqd,bhqskd->bhqsk', qq, kg,
                       preferred_element_type=jnp.float32) * scale
        s = jnp.where(mask, s, -jnp.inf).reshape(B, H, qc, -1)
        p = jax.nn.softmax(s, axis=-1).reshape(kpos.shape).astype(vg.dtype)
        return jnp.einsum('bhqsk,bhqskd->bhqd', p, vg)

    out = lax.map(one_chunk, (qs, pos))
    return out.transpose(1, 0, 3, 2, 4).reshape(B, Tq, H * Dh)


def depthwise_causal_conv(u_pad, w, b):
    y = lax.conv_general_dilated(u_pad, w[:, None, :], window_strides=(1,), padding='VALID',
                                 dimension_numbers=('NWC', 'WIO', 'NWC'),
                                 feature_group_count=u_pad.shape[-1])
    return y + b


def hybrid_layer(x, c, q_start, k_past, v_past, conv_prev, ada_w, ada_b, norm1_g, w_in,
                 conv_w, conv_b, conv_ln_g, conv_ln_b, w_attn_proj, w_conv_proj, w_out,
                 norm2_g, w_ff1, w_ff2):
    B, T, _ = x.shape
    mod = c @ ada_w + ada_b
    sh1, sc1, g1, sh2, sc2, g2 = jnp.split(mod[:, None, :], 6, axis=-1)
    h = rmsnorm(x, norm1_g) * (1 + sc1) + sh1
    z = h @ w_in
    cuts = [D_ATTN, 2 * D_ATTN, 3 * D_ATTN, 3 * D_ATTN + D_CONV, 3 * D_ATTN + 2 * D_CONV,
            3 * D_ATTN + 2 * D_CONV + D_MODEL]
    q, k, v, u, u_gate, g_attn, g_conv = jnp.split(z, cuts, axis=-1)
    pos = q_start + jnp.arange(T, dtype=jnp.int32)
    q = rope(q.reshape(B, T, N_HEADS, HEAD_DIM), pos)
    k = rope(k.reshape(B, T, N_HEADS, HEAD_DIM), pos)
    v = v.reshape(B, T, N_HEADS, HEAD_DIM)
    if k_past is None:
        k_all, v_all = k, v
    else:
        k_all = jnp.concatenate([k_past, k], axis=1)
        v_all = jnp.concatenate([v_past, v], axis=1)
    attn_out = moba_attention(q, k_all, v_all, q_start) @ w_attn_proj
    glu = u * jax.nn.sigmoid(u_gate)
    u_pad = jnp.concatenate([conv_prev.astype(glu.dtype), glu], axis=1)
    cv = depthwise_causal_conv(u_pad, conv_w, conv_b)
    cv = jax.nn.silu(layernorm(cv, conv_ln_g, conv_ln_b))
    conv_out = cv @ w_conv_proj
    mixed = (jax.nn.sigmoid(g_attn) * attn_out + jax.nn.sigmoid(g_conv) * conv_out) @ w_out
    x = x + g1 * mixed
    h2 = rmsnorm(x, norm2_g) * (1 + sc2) + sh2
    ff = jnp.square(jax.nn.relu(h2 @ w_ff1)) @ w_ff2
    x = x + g2 * ff
    return x, k, v, u_pad[:, -(CONV_W - 1):]


def setup_inputs(seed: int = 0) -> dict:
    key = jax.random.key(seed)
    ks = jax.random.split(key, 24)
    n_pages = PAST_LEN // PAGE_SIZE
    n_used = DEC_BATCH * n_pages
    n_pool = n_used + max(n_used // 4, 1)
    f32 = jnp.float32

    def nrm(k, shape, s):
        return jax.random.normal(k, shape, f32) * s

    inv = D_MODEL ** -0.5
    page_table = jax.random.permutation(ks[5], n_pool)[:n_used].reshape(DEC_BATCH, n_pages).astype(jnp.int32)
    return {
        "x_prompt": nrm(ks[0], (BATCH, SEQ, D_MODEL), 1.0),
        "x_sample": nrm(ks[1], (DEC_BATCH, DEC_SEQ, D_MODEL), 1.0),
        "cache_k": nrm(ks[2], (DEPTH, n_pool, PAGE_SIZE, N_HEADS, HEAD_DIM), 1.0),
        "cache_v": nrm(ks[3], (DEPTH, n_pool, PAGE_SIZE, N_HEADS, HEAD_DIM), 1.0),
        "state_conv": nrm(ks[4], (DEPTH, DEC_BATCH, CONV_W - 1, D_CONV), 0.5),
        "page_table": page_table,
        "c_prompt": nrm(ks[6], (BATCH, D_MODEL), 1.0),
        "c_sample": nrm(ks[7], (DEC_BATCH, D_MODEL), 1.0),
        "ada_w": nrm(ks[8], (DEPTH, D_MODEL, 6 * D_MODEL), 0.2 * inv),
        "ada_b": nrm(ks[9], (DEPTH, 6 * D_MODEL), 0.01),
        "norm1_g": 1.0 + nrm(ks[10], (DEPTH, D_MODEL), 0.05),
        "w_in": nrm(ks[11], (DEPTH, D_MODEL, D_IN), inv),
        "conv_w": nrm(ks[12], (DEPTH, CONV_W, D_CONV), CONV_W ** -0.5),
        "conv_b": nrm(ks[13], (DEPTH, D_CONV), 0.01),
        "conv_ln_g": 1.0 + nrm(ks[14], (DEPTH, D_CONV), 0.05),
        "conv_ln_b": nrm(ks[15], (DEPTH, D_CONV), 0.01),
        "w_attn_proj": nrm(ks[16], (DEPTH, D_ATTN, D_MODEL), D_ATTN ** -0.5),
        "w_conv_proj": nrm(ks[17], (DEPTH, D_CONV, D_MODEL), D_CONV ** -0.5),
        "w_out": nrm(ks[18], (DEPTH, D_MODEL, D_MODEL), inv),
        "norm2_g": 1.0 + nrm(ks[19], (DEPTH, D_MODEL), 0.05),
        "w_ff1": nrm(ks[20], (DEPTH, D_MODEL, D_FF), inv),
        "w_ff2": nrm(ks[21], (DEPTH, D_FF, D_MODEL), D_FF ** -0.5),
        "final_g": 1.0 + nrm(ks[22], (D_MODEL,), 0.05),
    }


def reference(x_prompt, x_sample, cache_k, cache_v, state_conv, page_table, c_prompt, c_sample,
              ada_w, ada_b, norm1_g, w_in, conv_w, conv_b, conv_ln_g, conv_ln_b,
              w_attn_proj, w_conv_proj, w_out, norm2_g, w_ff1, w_ff2, final_g):
    dec_b, n_pages = page_table.shape
    past_len = n_pages * PAGE_SIZE
    yp, ys = x_prompt, x_sample
    kp_l, vp_l, cp_l, ks_l, vs_l, cs_l = [], [], [], [], [], []
    for l in range(DEPTH):
        lw = (ada_w[l], ada_b[l], norm1_g[l], w_in[l], conv_w[l], conv_b[l], conv_ln_g[l],
              conv_ln_b[l], w_attn_proj[l], w_conv_proj[l], w_out[l], norm2_g[l], w_ff1[l], w_ff2[l])
        conv0 = jnp.zeros((yp.shape[0], CONV_W - 1, D_CONV), yp.dtype)
        yp, kp, vp, cp = hybrid_layer(yp, c_prompt, 0, None, None, conv0, *lw)
        k_past = cache_k[l][page_table].reshape(dec_b, past_len, N_HEADS, HEAD_DIM)
        v_past = cache_v[l][page_table].reshape(dec_b, past_len, N_HEADS, HEAD_DIM)
        ys, kn, vn, cn = hybrid_layer(ys, c_sample, past_len, k_past, v_past, state_conv[l], *lw)
        kp_l.append(kp); vp_l.append(vp); cp_l.append(cp)
        ks_l.append(kn); vs_l.append(vn); cs_l.append(cn)
    y_prompt = rmsnorm(yp, final_g)
    y_sample = rmsnorm(ys, final_g)
    k_prompt = jnp.stack(kp_l)
    v_prompt = jnp.stack(vp_l)
    conv_prompt = jnp.stack(cp_l)
    k_sample = jnp.stack(ks_l)
    v_sample = jnp.stack(vs_l)
    conv_sample = jnp.stack(cs_l)
    return (y_prompt, y_sample, k_prompt, v_prompt, conv_prompt, k_sample, v_sample, conv_sample)
```

```python
import functools

import jax
import jax.numpy as jnp
from jax import lax
from jax.experimental import pallas as pl
from jax.experimental.pallas import tpu as pltpu

MOBA_BLOCK = 256
MOBA_TOPK = 3
ROPE_THETA = 10000.0
EPS = 1e-6
LANES = 128
HALO = 32
VMEM_LIMIT = 56 * 1024 * 1024

F32 = jnp.float32
BF16 = jnp.bfloat16


def _pick(n, pref):
    if n <= pref:
        return n
    t = pref
    while n % t:
        t //= 2
    return t


def _params(n_axes):
    return pltpu.CompilerParams(dimension_semantics=("arbitrary",) * n_axes,
                                vmem_limit_bytes=VMEM_LIMIT)


def _dot(a, b):
    return jnp.dot(a, b, preferred_element_type=F32)


def _dot_nt(a, b):
    return lax.dot_general(a, b, (((1,), (1,)), ((), ())), preferred_element_type=F32)


def _row(ref):
    v = ref[...]
    return v.reshape(v.shape[-2:])


class _Mod:
    def __init__(self, arr, per_row, d_model, rows_per_batch):
        self.arr, self.per_row, self.d, self.rpb = arr, per_row, d_model, rows_per_batch

    def spec(self, which, tm, tn):
        if self.per_row:
            nj = self.d // tn
            return pl.BlockSpec((tm, tn), lambda i, j, *_: (i, which * nj + j))
        tpb = self.rpb // tm
        return pl.BlockSpec((1, 1, tn), lambda i, j, *_: ((i // tpb) * 6 + which, 0, j))


def _ada_kernel(c_ref, w_ref, b_ref, o_ref):
    o_ref[...] = _dot(c_ref[...].astype(BF16), w_ref[...].astype(BF16)) + b_ref[...]


def _ada(c, w, b):
    r, d = c.shape
    n = w.shape[1]
    tn = _pick(n, 512)
    return pl.pallas_call(
        _ada_kernel, grid=(n // tn,),
        in_specs=[pl.BlockSpec((r, d), lambda j: (0, 0)),
                  pl.BlockSpec((d, tn), lambda j: (0, j)),
                  pl.BlockSpec((1, tn), lambda j: (0, j))],
        out_specs=pl.BlockSpec((r, tn), lambda j: (0, j)),
        out_shape=jax.ShapeDtypeStruct((r, n), F32),
        compiler_params=_params(1), name="ada_mod")(c, w, b.reshape(1, n))


def _norm_mod_kernel(x_ref, g_ref, sc_ref, sh_ref, o_ref):
    x = x_ref[...]
    y = x * lax.rsqrt(jnp.mean(x * x, axis=-1, keepdims=True) + EPS) * g_ref[...]
    o_ref[...] = (y * (1.0 + _row(sc_ref)) + _row(sh_ref)).astype(o_ref.dtype)


def _norm_mod(x, g, mod, which_sc, which_sh):
    m, d = x.shape
    tm = _pick(mod.rpb, 256)
    return pl.pallas_call(
        _norm_mod_kernel, grid=(m // tm, 1),
        in_specs=[pl.BlockSpec((tm, d), lambda i, j: (i, 0)),
                  pl.BlockSpec((1, d), lambda i, j: (0, 0)),
                  mod.spec(which_sc, tm, d), mod.spec(which_sh, tm, d)],
        out_specs=pl.BlockSpec((tm, d), lambda i, j: (i, 0)),
        out_shape=jax.ShapeDtypeStruct((m, d), BF16),
        compiler_params=_params(2), name="norm_mod")(x, g.reshape(1, d), mod.arr, mod.arr)


def _final_norm_kernel(x_ref, g_ref, o_ref):
    x = x_ref[...]
    o_ref[...] = x * lax.rsqrt(jnp.mean(x * x, axis=-1, keepdims=True) + EPS) * g_ref[...]


def _final_norm(x, g):
    m, d = x.shape
    tm = _pick(m, 256)
    return pl.pallas_call(
        _final_norm_kernel, grid=(m // tm,),
        in_specs=[pl.BlockSpec((tm, d), lambda i: (i, 0)), pl.BlockSpec((1, d), lambda i: (0, 0))],
        out_specs=pl.BlockSpec((tm, d), lambda i: (i, 0)),
        out_shape=jax.ShapeDtypeStruct((m, d), F32),
        compiler_params=_params(1), name="final_norm")(x, g.reshape(1, d))


def _proj_kernel(*refs, rope, head_dim):
    a_ref, w_ref = refs[:2]
    outs = refs[4:] if rope else refs[2:]
    acc = _dot(a_ref[...], w_ref[...])
    if not rope:
        for o in outs:
            o[...] = acc.astype(o.dtype)
        return
    cos, sin = refs[2][...], refs[3][...]
    for c in range(acc.shape[1] // head_dim):
        xh = acc[:, c * head_dim:(c + 1) * head_dim]
        r = xh * cos + pltpu.roll(xh, head_dim // 2, 1) * sin
        for o in outs:
            o[:, c * head_dim:(c + 1) * head_dim] = r.astype(o.dtype)


def _proj(a, w, col_off, n, out_dtypes, rpb, rope_tabs=None, head_dim=LANES):
    m, k = a.shape
    tm, tn = _pick(rpb, 1024), _pick(n, 512)
    in_specs = [pl.BlockSpec((tm, k), lambda i, j: (i, 0)),
                pl.BlockSpec((k, tn), lambda i, j: (0, col_off // tn + j))]
    args = [a, w]
    if rope_tabs is not None:
        tpb = rope_tabs[0].shape[0] // tm
        tab_spec = pl.BlockSpec((tm, head_dim), lambda i, j: (i % tpb, 0))
        in_specs += [tab_spec, tab_spec]
        args += list(rope_tabs)
    return pl.pallas_call(
        functools.partial(_proj_kernel, rope=rope_tabs is not None, head_dim=head_dim),
        grid=(m // tm, n // tn), in_specs=in_specs,
        out_specs=[pl.BlockSpec((tm, tn), lambda i, j: (i, j)) for _ in out_dtypes],
        out_shape=[jax.ShapeDtypeStruct((m, n), dt) for dt in out_dtypes],
        compiler_params=_params(2), name="in_proj")(*args)


def _glu_kernel(a_ref, wu_ref, wg_ref, o_ref):
    a = a_ref[...]
    o_ref[...] = _dot(a, wu_ref[...]) * jax.nn.sigmoid(_dot(a, wg_ref[...]))


def _glu_proj(a, w, off_u, off_g, n):
    m, k = a.shape
    tm, tn = _pick(m, 1024), _pick(n, 512)
    return pl.pallas_call(
        _glu_kernel, grid=(m // tm, n // tn),
        in_specs=[pl.BlockSpec((tm, k), lambda i, j: (i, 0)),
                  pl.BlockSpec((k, tn), lambda i, j: (0, off_u // tn + j)),
                  pl.BlockSpec((k, tn), lambda i, j: (0, off_g // tn + j))],
        out_specs=pl.BlockSpec((tm, tn), lambda i, j: (i, j)),
        out_shape=jax.ShapeDtypeStruct((m, n), F32),
        compiler_params=_params(2), name="glu_proj")(a, w, w)


def _conv_kernel(prev_ref, cur_ref, w_ref, b_ref, g_ref, beta_ref, o_ref, pad_ref, conv_ref, *,
                 zero_first, taps):
    tt, ch = cur_ref.shape[1], cur_ref.shape[2]
    prev_rows = prev_ref.shape[1]
    prev = prev_ref[0]
    if zero_first:
        prev = jnp.where(pl.program_id(1) > 0, prev, 0.0)
    pad_ref[HALO - prev_rows:HALO, :] = prev
    pad_ref[HALO:HALO + tt, :] = cur_ref[0]
    first = HALO - (taps - 1)
    rc, cc = min(32, tt), min(256, ch)
    for r0 in range(0, tt, rc):
        for c0 in range(0, ch, cc):
            acc = jnp.broadcast_to(b_ref[:, c0:c0 + cc], (rc, cc))
            for t in range(taps):
                acc = acc + pad_ref[first + r0 + t:first + r0 + t + rc, c0:c0 + cc] * w_ref[t:t + 1, c0:c0 + cc]
            conv_ref[r0:r0 + rc, c0:c0 + cc] = acc
    lr = min(16, tt)
    g, beta = g_ref[...], beta_ref[...]
    for r0 in range(0, tt, lr):
        y = conv_ref[r0:r0 + lr, :]
        d = y - jnp.mean(y, axis=-1, keepdims=True)
        z = d * lax.rsqrt(jnp.mean(d * d, axis=-1, keepdims=True) + EPS) * g + beta
        o_ref[0, r0:r0 + lr, :] = (z * jax.nn.sigmoid(z)).astype(o_ref.dtype)


def _conv_module(glu, prev, conv_w, conv_b, ln_g, ln_b, out_dtype):
    b, t, ch = glu.shape
    taps = conv_w.shape[0]
    if prev is None:
        tt = _pick(t, 128)
        hb = tt // HALO
        prev_arr = glu
        prev_spec = pl.BlockSpec((1, HALO, ch), lambda bi, i: (bi, jnp.maximum(i * hb - 1, 0), 0))
    else:
        tt = t
        prev_arr = prev
        prev_spec = pl.BlockSpec((1, taps - 1, ch), lambda bi, i: (bi, 0, 0))
    vec = pl.BlockSpec((1, ch), lambda bi, i: (0, 0))
    return pl.pallas_call(
        functools.partial(_conv_kernel, zero_first=prev is None, taps=taps),
        grid=(b, t // tt),
        in_specs=[prev_spec, pl.BlockSpec((1, tt, ch), lambda bi, i: (bi, i, 0)),
                  pl.BlockSpec((taps, ch), lambda bi, i: (0, 0)), vec, vec, vec],
        out_specs=pl.BlockSpec((1, tt, ch), lambda bi, i: (bi, i, 0)),
        out_shape=jax.ShapeDtypeStruct((b, t, ch), out_dtype),
        scratch_shapes=[pltpu.VMEM((HALO + tt, ch), F32), pltpu.VMEM((tt, ch), F32)],
        compiler_params=_params(2), name="conv_module")(
            prev_arr, glu, conv_w, conv_b.reshape(1, ch), ln_g.reshape(1, ch), ln_b.reshape(1, ch))


def _moba_prompt_kernel(q_ref, k_ref, v_ref, o_ref, kmean_ref, *, n_blocks, scale):
    blk = MOBA_BLOCK
    qi = pl.program_id(2)

    @pl.when(qi == 0)
    def _():
        kmean_ref[...] = jnp.zeros_like(kmean_ref)
        for n in range(n_blocks):
            kb = k_ref[n * blk:(n + 1) * blk, :].astype(F32)
            kmean_ref[n:n + 1, :] = jnp.sum(kb, axis=0, keepdims=True) * (1.0 / blk)

    q = q_ref[...]
    km = kmean_ref[...]
    km_hi = km.astype(BF16)
    km_lo = (km - km_hi.astype(F32)).astype(BF16)
    gate = _dot_nt(q, km_hi) + _dot_nt(q, km_lo)
    lane = lax.broadcasted_iota(jnp.int32, gate.shape, 1)
    rank = jnp.zeros(gate.shape, jnp.int32)
    for m in range(n_blocks):
        gm = gate[:, m:m + 1]
        beats = (gm > gate) | ((gm == gate) & (m < lane))
        rank = rank + jnp.where(beats, jnp.where(m < qi, 1, 0), 0)
    sel = jnp.where((lane < qi) & (rank < MOBA_TOPK), 1.0, 0.0)

    row = lax.broadcasted_iota(jnp.int32, (blk, blk), 0)
    col = lax.broadcasted_iota(jnp.int32, (blk, blk), 1)
    d0 = pl.multiple_of(qi * blk, blk)
    s = _dot_nt(q, k_ref[pl.ds(d0, blk), :]) * scale
    s = jnp.where(col <= row, s, -jnp.inf)
    m0 = jnp.max(s, axis=1, keepdims=True)
    p = jnp.exp(s - m0)
    l0 = jnp.sum(p, axis=1, keepdims=True)
    acc0 = _dot(p.astype(BF16), v_ref[pl.ds(d0, blk), :])

    def body(j, carry):
        m_i, l_i, acc = carry
        j0 = pl.multiple_of(j * blk, blk)
        s = _dot_nt(q, k_ref[pl.ds(j0, blk), :]) * scale
        keep = jnp.sum(jnp.where(lane == j, sel, 0.0), axis=1, keepdims=True) > 0.0
        s = jnp.where(keep, s, -jnp.inf)
        m_new = jnp.maximum(m_i, jnp.max(s, axis=1, keepdims=True))
        alpha = jnp.exp(m_i - m_new)
        p = jnp.exp(s - m_new)
        l_new = alpha * l_i + jnp.sum(p, axis=1, keepdims=True)
        acc = alpha * acc + _dot(p.astype(BF16), v_ref[pl.ds(j0, blk), :])
        return m_new, l_new, acc

    _, l_i, acc = lax.fori_loop(0, qi, body, (m0, l0, acc0))
    o_ref[...] = (acc / l_i).astype(o_ref.dtype)


def _moba_prompt(q, k, v, batch, seq, n_heads, head_dim):
    assert seq % MOBA_BLOCK == 0 and seq // MOBA_BLOCK <= LANES
    nb = seq // MOBA_BLOCK
    kv_spec = pl.BlockSpec((seq, head_dim), lambda b, h, i: (b, h))
    q_spec = pl.BlockSpec((MOBA_BLOCK, head_dim), lambda b, h, i: (b * nb + i, h))
    return pl.pallas_call(
        functools.partial(_moba_prompt_kernel, n_blocks=nb, scale=head_dim ** -0.5),
        grid=(batch, n_heads, nb),
        in_specs=[q_spec, kv_spec, kv_spec], out_specs=q_spec,
        out_shape=jax.ShapeDtypeStruct(q.shape, BF16),
        scratch_shapes=[pltpu.VMEM((LANES, head_dim), F32)],
        compiler_params=_params(3), name="moba_prompt")(q, k, v)


def _kmean_kernel(pt_ref, k_ref, o_ref, *, ppb):
    p = pl.program_id(1) % ppb
    s = jnp.sum(k_ref[0], axis=0, keepdims=True)

    @pl.when(p == 0)
    def _():
        o_ref[0, 0] = s

    @pl.when(p > 0)
    def _():
        o_ref[0, 0] += s

    @pl.when(p == ppb - 1)
    def _():
        o_ref[0, 0] = o_ref[0, 0] * (1.0 / MOBA_BLOCK)


def _paged_kmean(cache3, pt_flat, bs, n_pages):
    n_pool, page, hd = cache3.shape
    ppb = MOBA_BLOCK // page
    nbp = n_pages // ppb
    out = pl.pallas_call(
        functools.partial(_kmean_kernel, ppb=ppb),
        grid_spec=pltpu.PrefetchScalarGridSpec(
            num_scalar_prefetch=1, grid=(bs, n_pages),
            in_specs=[pl.BlockSpec((1, page, hd), lambda b, p, pt: (pt[b * n_pages + p], 0, 0))],
            out_specs=pl.BlockSpec((1, 1, 1, hd), lambda b, p, pt: (b, p // ppb, 0, 0))),
        out_shape=jax.ShapeDtypeStruct((bs, nbp, 1, hd), F32),
        compiler_params=_params(2), name="paged_kmean")(pt_flat, cache3)
    return out.reshape(bs, nbp, hd)


def _select_kernel(q_ref, km_ref, seg_ref, o_ref, *, n_heads):
    km = km_ref[0]
    nbp = km.shape[0]
    seg = seg_ref[...]
    blk_id = lax.broadcasted_iota(jnp.int32, (nbp, LANES), 0)
    for t in range(q_ref.shape[1]):
        prod = km * q_ref[0, t:t + 1, :]
        p1 = prod.astype(BF16)
        r1 = prod - p1.astype(F32)
        p2 = r1.astype(BF16)
        p3 = (r1 - p2.astype(F32)).astype(BF16)
        gate = _dot(p1, seg) + _dot(p2, seg) + _dot(p3, seg)
        rank = jnp.zeros(gate.shape, jnp.int32)
        for m in range(nbp):
            gm = gate[m:m + 1, :]
            beats = (gm > gate) | ((gm == gate) & (m < blk_id))
            rank = rank + jnp.where(beats, 1, 0)
        for r in range(MOBA_TOPK):
            idx = jnp.sum(jnp.where(rank == r, blk_id, 0), axis=0, keepdims=True)
            o_ref[0, t, r:r + 1, :] = jnp.clip(idx, 0, nbp - 1)[:, :n_heads]


def _select_blocks(q3, kmean, n_heads, head_dim):
    bs, tq, hd = q3.shape
    nbp = kmean.shape[1]
    assert nbp >= MOBA_TOPK and n_heads <= LANES
    seg = (jnp.arange(hd)[:, None] // head_dim == jnp.arange(LANES)[None, :]).astype(BF16)
    return pl.pallas_call(
        functools.partial(_select_kernel, n_heads=n_heads),
        grid=(bs,),
        in_specs=[pl.BlockSpec((1, tq, hd), lambda b: (b, 0, 0)),
                  pl.BlockSpec((1, nbp, hd), lambda b: (b, 0, 0)),
                  pl.BlockSpec((hd, LANES), lambda b: (0, 0))],
        out_specs=pl.BlockSpec((1, tq, MOBA_TOPK, n_heads), lambda b: (b, 0, 0, 0)),
        out_shape=jax.ShapeDtypeStruct((bs, tq, MOBA_TOPK, n_heads), jnp.int32),
        compiler_params=_params(1), name="moba_select")(q3, kmean, seg)


def _sample_attn_kernel(pt_ref, sel_ref, q_ref, kn_ref, vn_ref, *rest, tq, ppb, scale):
    nsl = tq * MOBA_TOPK * ppb
    k_refs, v_refs, o_ref = rest[:nsl], rest[nsl:2 * nsl], rest[2 * nsl]
    q, kn, vn = q_ref[0], kn_ref[0], vn_ref[0]
    new_id = lax.broadcasted_iota(jnp.int32, (tq, 1), 0)
    for t in range(tq):
        qrow = q[t:t + 1, :]
        s_new = jnp.sum(kn * qrow, axis=1, keepdims=True) * scale
        s_new = jnp.where(new_id <= t, s_new, -jnp.inf)
        scores = [jnp.sum(k_refs[t * MOBA_TOPK * ppb + i][0] * qrow, axis=1, keepdims=True) * scale
                  for i in range(MOBA_TOPK * ppb)]
        m = jnp.max(s_new, axis=0, keepdims=True)
        for s in scores:
            m = jnp.maximum(m, jnp.max(s, axis=0, keepdims=True))
        p_new = jnp.exp(s_new - m)
        l = jnp.sum(p_new, axis=0, keepdims=True)
        acc = jnp.sum(p_new * vn, axis=0, keepdims=True)
        for i, s in enumerate(scores):
            p = jnp.exp(s - m)
            l = l + jnp.sum(p, axis=0, keepdims=True)
            acc = acc + jnp.sum(p * v_refs[t * MOBA_TOPK * ppb + i][0], axis=0, keepdims=True)
        o_ref[0, t:t + 1, :] = acc / l


def _sample_attn(q3, kn3, vn3, cache_k3, cache_v3, pt_flat, sel_flat, n_pages, n_heads, head_dim):
    bs, tq, hd = q3.shape
    page = cache_k3.shape[1]
    ppb = MOBA_BLOCK // page

    def slab_spec(t, r, half):
        def index_map(b, h, pt, sel):
            blk = sel[((b * tq + t) * MOBA_TOPK + r) * n_heads + h]
            return (pt[b * n_pages + blk * ppb + half], 0, h)
        return pl.BlockSpec((1, page, head_dim), index_map)

    slabs = [slab_spec(t, r, half) for t in range(tq) for r in range(MOBA_TOPK) for half in range(ppb)]
    tok = pl.BlockSpec((1, tq, head_dim), lambda b, h, pt, sel: (b, 0, h))
    return pl.pallas_call(
        functools.partial(_sample_attn_kernel, tq=tq, ppb=ppb, scale=head_dim ** -0.5),
        grid_spec=pltpu.PrefetchScalarGridSpec(
            num_scalar_prefetch=2, grid=(bs, n_heads),
            in_specs=[tok, tok, tok] + slabs + slabs, out_specs=tok),
        out_shape=jax.ShapeDtypeStruct((bs, tq, hd), F32),
        compiler_params=_params(2), name="moba_sample")(
            pt_flat, sel_flat, q3, kn3, vn3, *([cache_k3] * len(slabs)), *([cache_v3] * len(slabs)))


def _merge_kernel(h_ref, at_ref, cv_ref, wga_ref, wgc_ref, wap_ref, wcp_ref, o_ref):
    h = h_ref[...]
    ga = jax.nn.sigmoid(_dot(h, wga_ref[...]))
    gc = jax.nn.sigmoid(_dot(h, wgc_ref[...]))
    o_ref[...] = (ga * _dot(at_ref[...], wap_ref[...]) + gc * _dot(cv_ref[...], wcp_ref[...])).astype(o_ref.dtype)


def _merge(h, attn, cv, w_in, off_ga, off_gc, w_ap, w_cp):
    m, d = h.shape
    n = w_ap.shape[1]
    tm, tn = _pick(m, 512), _pick(n, 512)
    return pl.pallas_call(
        _merge_kernel, grid=(m // tm, n // tn),
        in_specs=[pl.BlockSpec((tm, d), lambda i, j: (i, 0)),
                  pl.BlockSpec((tm, attn.shape[1]), lambda i, j: (i, 0)),
                  pl.BlockSpec((tm, cv.shape[1]), lambda i, j: (i, 0)),
                  pl.BlockSpec((d, tn), lambda i, j: (0, off_ga // tn + j)),
                  pl.BlockSpec((d, tn), lambda i, j: (0, off_gc // tn + j)),
                  pl.BlockSpec((w_ap.shape[0], tn), lambda i, j: (0, j)),
                  pl.BlockSpec((w_cp.shape[0], tn), lambda i, j: (0, j))],
        out_specs=pl.BlockSpec((tm, tn), lambda i, j: (i, j)),
        out_shape=jax.ShapeDtypeStruct((m, n), BF16),
        compiler_params=_params(2), name="gated_merge")(h, attn, cv, w_in, w_in, w_ap, w_cp)


def _resid_kernel(a_ref, w_ref, x_ref, g_ref, o_ref):
    o_ref[...] = x_ref[...] + _row(g_ref) * _dot(a_ref[...], w_ref[...])


def _resid_proj(a, w, x, mod, which):
    m, k = a.shape
    n = w.shape[1]
    tm, tn = _pick(mod.rpb, 1024), _pick(n, 512)
    return pl.pallas_call(
        _resid_kernel, grid=(m // tm, n // tn),
        in_specs=[pl.BlockSpec((tm, k), lambda i, j: (i, 0)),
                  pl.BlockSpec((k, tn), lambda i, j: (0, j)),
                  pl.BlockSpec((tm, tn), lambda i, j: (i, j)),
                  mod.spec(which, tm, tn)],
        out_specs=pl.BlockSpec((tm, tn), lambda i, j: (i, j)),
        out_shape=jax.ShapeDtypeStruct((m, n), F32),
        compiler_params=_params(2), name="out_proj")(a, w, x, mod.arr)


def _ff1_kernel(a_ref, w_ref, o_ref):
    r = jnp.maximum(_dot(a_ref[...], w_ref[...]), 0.0)
    o_ref[...] = (r * r).astype(o_ref.dtype)


def _ff1(a, w):
    m, k = a.shape
    n = w.shape[1]
    tm, tn = _pick(m, 1024), _pick(n, 1024)
    return pl.pallas_call(
        _ff1_kernel, grid=(m // tm, n // tn),
        in_specs=[pl.BlockSpec((tm, k), lambda i, j: (i, 0)), pl.BlockSpec((k, tn), lambda i, j: (0, j))],
        out_specs=pl.BlockSpec((tm, tn), lambda i, j: (i, j)),
        out_shape=jax.ShapeDtypeStruct((m, n), BF16),
        compiler_params=_params(2), name="ff1")(a, w)


def _ff2_kernel(a_ref, w_ref, x_ref, g_ref, o_ref, acc_ref):
    kk = pl.program_id(2)

    @pl.when(kk == 0)
    def _():
        acc_ref[...] = jnp.zeros_like(acc_ref)

    acc_ref[...] += _dot(a_ref[...], w_ref[...])

    @pl.when(kk == pl.num_programs(2) - 1)
    def _():
        o_ref[...] = x_ref[...] + _row(g_ref) * acc_ref[...]


def _ff2(a, w, x, mod, which):
    m, k = a.shape
    n = w.shape[1]
    tm, tn, tk = _pick(mod.rpb, 1024), _pick(n, 1024), _pick(k, 2048)
    return pl.pallas_call(
        _ff2_kernel, grid=(m // tm, n // tn, k // tk),
        in_specs=[pl.BlockSpec((tm, tk), lambda i, j, kk: (i, kk)),
                  pl.BlockSpec((tk, tn), lambda i, j, kk: (kk, j)),
                  pl.BlockSpec((tm, tn), lambda i, j, kk: (i, j)),
                  mod.spec(which, tm, tn)],
        out_specs=pl.BlockSpec((tm, tn), lambda i, j, kk: (i, j)),
        out_shape=jax.ShapeDtypeStruct((m, n), F32),
        scratch_shapes=[pltpu.VMEM((tm, tn), F32)],
        compiler_params=_params(3), name="ff2")(a, w, x, mod.arr)


def _rope_tables(pos, head_dim):
    half = head_dim // 2
    inv_freq = ROPE_THETA ** (-jnp.arange(half, dtype=F32) / half)
    ang = pos.astype(F32)[:, None] * inv_freq[None, :]
    cos, sin = jnp.cos(ang), jnp.sin(ang)
    return jnp.concatenate([cos, cos], axis=1), jnp.concatenate([-sin, sin], axis=1)


def _layer(x, mod, pos_rows, group, lw, n_heads, head_dim, d_conv):
    (norm1_g, w_in, conv_w, conv_b, ln_g, ln_b, w_ap, w_cp, w_out, norm2_g, w_ff1, w_ff2) = lw
    m, d = x.shape
    d_attn = n_heads * head_dim
    off_k, off_v, off_u = d_attn, 2 * d_attn, 3 * d_attn
    off_ug, off_ga = off_u + d_conv, off_u + 2 * d_conv
    off_gc = off_ga + d
    tabs = _rope_tables(pos_rows, head_dim)

    h = _norm_mod(x, norm1_g, mod, 1, 0)
    glu = _glu_proj(h, w_in, off_u, off_ug, d_conv)
    if group[0] == "prompt":
        _, batch, seq = group
        (q_b,) = _proj(h, w_in, 0, d_attn, [BF16], seq, tabs, head_dim)
        k_f, k_b = _proj(h, w_in, off_k, d_attn, [F32, BF16], seq, tabs, head_dim)
        v_f, v_b = _proj(h, w_in, off_v, d_attn, [F32, BF16], seq)
        attn = _moba_prompt(q_b, k_b, v_b, batch, seq, n_heads, head_dim)
        glu3 = glu.reshape(batch, seq, d_conv)
        cv = _conv_module(glu3, None, conv_w, conv_b, ln_g, ln_b, BF16).reshape(m, d_conv)
        conv_state = glu3[:, seq - (conv_w.shape[0] - 1):]
    else:
        _, cache_k3, cache_v3, state, page_table = group
        bs, n_pages = page_table.shape
        tq = m // bs
        (q_f,) = _proj(h, w_in, 0, d_attn, [F32], m, tabs, head_dim)
        (k_f,) = _proj(h, w_in, off_k, d_attn, [F32], m, tabs, head_dim)
        (v_f,) = _proj(h, w_in, off_v, d_attn, [F32], m)
        pt_flat = page_table.reshape(-1)
        q3 = q_f.reshape(bs, tq, d_attn)
        kmean = _paged_kmean(cache_k3, pt_flat, bs, n_pages)
        sel = _select_blocks(q3, kmean, n_heads, head_dim)
        attn = _sample_attn(q3, k_f.reshape(bs, tq, d_attn), v_f.reshape(bs, tq, d_attn), cache_k3, cache_v3,
                            pt_flat, sel.reshape(-1), n_pages, n_heads, head_dim)
        attn = attn.reshape(m, d_attn).astype(BF16)
        glu3 = glu.reshape(bs, tq, d_conv)
        cv = _conv_module(glu3, state, conv_w, conv_b, ln_g, ln_b, F32).reshape(m, d_conv).astype(BF16)
        conv_state = jnp.concatenate([state, glu3], axis=1)[:, tq:]

    mixed = _merge(h, attn, cv, w_in, off_ga, off_gc, w_ap, w_cp)
    x1 = _resid_proj(mixed, w_out, x, mod, 2)
    h2 = _norm_mod(x1, norm2_g, mod, 4, 3)
    x2 = _ff2(_ff1(h2, w_ff1), w_ff2, x1, mod, 5)
    return x2, k_f, v_f, conv_state


def kernel(x_prompt, x_sample, cache_k, cache_v, state_conv, page_table, c_prompt, c_sample, ada_w, ada_b,
           norm1_g, w_in, conv_w, conv_b, conv_ln_g, conv_ln_b, w_attn_proj, w_conv_proj, w_out, norm2_g,
           w_ff1, w_ff2, final_g):
    bp, seq, d = x_prompt.shape
    bs, tq, _ = x_sample.shape
    depth, n_pool, page, n_heads, head_dim = cache_k.shape
    d_conv = state_conv.shape[-1]
    n_pages = page_table.shape[1]
    assert MOBA_BLOCK % page == 0 and (n_pages * page) % MOBA_BLOCK == 0 and tq <= MOBA_BLOCK

    yp = x_prompt.reshape(bp * seq, d)
    ys = x_sample.reshape(bs * tq, d)
    n_c = bp + bs
    c_rows = -(-n_c // 8) * 8
    c_all = jnp.concatenate([c_prompt, c_sample, jnp.zeros((c_rows - n_c, d), F32)], axis=0)
    pos_p = jnp.arange(seq, dtype=jnp.int32)
    pos_s = n_pages * page + jnp.tile(jnp.arange(tq, dtype=jnp.int32), bs)

    outs = [[] for _ in range(6)]
    for l in range(depth):
        mod_all = _ada(c_all, ada_w[l], ada_b[l])
        mod_p = _Mod(mod_all[:bp].reshape(bp * 6, 1, d), False, d, seq)
        mod_s = _Mod(jnp.repeat(mod_all[bp:n_c], tq, axis=0), True, d, bs * tq)
        lw = (norm1_g[l], w_in[l].astype(BF16), conv_w[l], conv_b[l], conv_ln_g[l], conv_ln_b[l],
              w_attn_proj[l].astype(BF16), w_conv_proj[l].astype(BF16), w_out[l].astype(BF16), norm2_g[l],
              w_ff1[l].astype(BF16), w_ff2[l].astype(BF16))
        yp, kp, vp, cp = _layer(yp, mod_p, pos_p, ("prompt", bp, seq), lw, n_heads, head_dim, d_conv)
        group_s = ("sample", cache_k[l].reshape(n_pool, page, n_heads * head_dim),
                   cache_v[l].reshape(n_pool, page, n_heads * head_dim), state_conv[l], page_table)
        ys, kn, vn, cn = _layer(ys, mod_s, pos_s, group_s, lw, n_heads, head_dim, d_conv)
        for lst, val in zip(outs, (kp.reshape(bp, seq, n_heads, head_dim), vp.reshape(bp, seq, n_heads, head_dim), cp,
                                   kn.reshape(bs, tq, n_heads, head_dim), vn.reshape(bs, tq, n_heads, head_dim), cn)):
            lst.append(val)
    y_prompt = _final_norm(yp, final_g).reshape(bp, seq, d)
    y_sample = _final_norm(ys, final_g).reshape(bs, tq, d)
    return (y_prompt, y_sample) + tuple(jnp.stack(o) for o in outs)
```

```python
import functools

import jax
import jax.numpy as jnp
from jax import lax
from jax.experimental import pallas as pl
from jax.experimental.pallas import tpu as pltpu

MOBA_BLOCK = 256
MOBA_TOPK = 3
ROPE_THETA = 10000.0
EPS = 1e-6
LANES = 128
HALO = 32
VMEM_LIMIT = 56 * 1024 * 1024

F32 = jnp.float32
BF16 = jnp.bfloat16


def _pick(n, pref):
    if n <= pref:
        return n
    t = pref
    while n % t:
        t //= 2
    return t


def _params(n_axes):
    return pltpu.CompilerParams(dimension_semantics=("arbitrary",) * n_axes,
                                vmem_limit_bytes=VMEM_LIMIT)


def _dot(a, b):
    return jnp.dot(a, b, preferred_element_type=F32)


def _dot_nt(a, b):
    return lax.dot_general(a, b, (((1,), (1,)), ((), ())), preferred_element_type=F32)


def _row(ref):
    v = ref[...]
    return v.reshape(v.shape[-2:])


class _Mod:
    def __init__(self, arr, per_row, d_model, rows_per_batch):
        self.arr, self.per_row, self.d, self.rpb = arr, per_row, d_model, rows_per_batch

    def spec(self, which, tm, tn):
        if self.per_row:
            nj = self.d // tn
            return pl.BlockSpec((tm, tn), lambda i, j, *_: (i, which * nj + j))
        tpb = self.rpb // tm
        return pl.BlockSpec((1, 1, tn), lambda i, j, *_: ((i // tpb) * 6 + which, 0, j))


def _ada_kernel(c_ref, w_ref, b_ref, o_ref):
    o_ref[...] = _dot(c_ref[...].astype(BF16), w_ref[...].astype(BF16)) + b_ref[...]


def _ada(c, w, b):
    r, d = c.shape
    n = w.shape[1]
    tn = _pick(n, 512)
    return pl.pallas_call(
        _ada_kernel, grid=(n // tn,),
        in_specs=[pl.BlockSpec((r, d), lambda j: (0, 0)),
                  pl.BlockSpec((d, tn), lambda j: (0, j)),
                  pl.BlockSpec((1, tn), lambda j: (0, j))],
        out_specs=pl.BlockSpec((r, tn), lambda j: (0, j)),
        out_shape=jax.ShapeDtypeStruct((r, n), F32),
        compiler_params=_params(1), name="ada_mod")(c, w, b.reshape(1, n))


def _norm_mod_kernel(x_ref, g_ref, sc_ref, sh_ref, o_ref):
    x = x_ref[...]
    y = x * lax.rsqrt(jnp.mean(x * x, axis=-1, keepdims=True) + EPS) * g_ref[...]
    o_ref[...] = (y * (1.0 + _row(sc_ref)) + _row(sh_ref)).astype(o_ref.dtype)


def _norm_mod(x, g, mod, which_sc, which_sh):
    m, d = x.shape
    tm = _pick(mod.rpb, 256)
    return pl.pallas_call(
        _norm_mod_kernel, grid=(m // tm, 1),
        in_specs=[pl.BlockSpec((tm, d), lambda i, j: (i, 0)),
                  pl.BlockSpec((1, d), lambda i, j: (0, 0)),
                  mod.spec(which_sc, tm, d), mod.spec(which_sh, tm, d)],
        out_specs=pl.BlockSpec((tm, d), lambda i, j: (i, 0)),
        out_shape=jax.ShapeDtypeStruct((m, d), BF16),
        compiler_params=_params(2), name="norm_mod")(x, g.reshape(1, d), mod.arr, mod.arr)


def _final_norm_kernel(x_ref, g_ref, o_ref):
    x = x_ref[...]
    o_ref[...] = x * lax.rsqrt(jnp.mean(x * x, axis=-1, keepdims=True) + EPS) * g_ref[...]


def _final_norm(x, g):
    m, d = x.shape
    tm = _pick(m, 256)
    return pl.pallas_call(
        _final_norm_kernel, grid=(m // tm,),
        in_specs=[pl.BlockSpec((tm, d), lambda i: (i, 0)), pl.BlockSpec((1, d), lambda i: (0, 0))],
        out_specs=pl.BlockSpec((tm, d), lambda i: (i, 0)),
        out_shape=jax.ShapeDtypeStruct((m, d), F32),
        compiler_params=_params(1), name="final_norm")(x, g.reshape(1, d))


def _proj_kernel(*refs, rope, head_dim):
    a_ref, w_ref = refs[:2]
    outs = refs[4:] if rope else refs[2:]
    acc = _dot(a_ref[...], w_ref[...])
    if not rope:
        for o in outs:
            o[...] = acc.astype(o.dtype)
        return
    cos, sin = refs[2][...], refs[3][...]
    for c in range(acc.shape[1] // head_dim):
        xh = acc[:, c * head_dim:(c + 1) * head_dim]
        r = xh * cos + pltpu.roll(xh, head_dim // 2, 1) * sin
        for o in outs:
            o[:, c * head_dim:(c + 1) * head_dim] = r.astype(o.dtype)


def _proj(a, w, col_off, n, out_dtypes, rpb, rope_tabs=None, head_dim=LANES):
    m, k = a.shape
    tm, tn = _pick(rpb, 1024), _pick(n, 512)
    in_specs = [pl.BlockSpec((tm, k), lambda i, j: (i, 0)),
                pl.BlockSpec((k, tn), lambda i, j: (0, col_off // tn + j))]
    args = [a, w]
    if rope_tabs is not None:
        tpb = rope_tabs[0].shape[0] // tm
        tab_spec = pl.BlockSpec((tm, head_dim), lambda i, j: (i % tpb, 0))
        in_specs += [tab_spec, tab_spec]
        args += list(rope_tabs)
    return pl.pallas_call(
        functools.partial(_proj_kernel, rope=rope_tabs is not None, head_dim=head_dim),
        grid=(m // tm, n // tn), in_specs=in_specs,
        out_specs=[pl.BlockSpec((tm, tn), lambda i, j: (i, j)) for _ in out_dtypes],
        out_shape=[jax.ShapeDtypeStruct((m, n), dt) for dt in out_dtypes],
        compiler_params=_params(2), name="in_proj")(*args)


def _glu_kernel(a_ref, wu_ref, wg_ref, o_ref):
    a = a_ref[...]
    o_ref[...] = _dot(a, wu_ref[...]) * jax.nn.sigmoid(_dot(a, wg_ref[...]))


def _glu_proj(a, w, off_u, off_g, n):
    m, k = a.shape
    tm, tn = _pick(m, 1024), _pick(n, 512)
    return pl.pallas_call(
        _glu_kernel, grid=(m // tm, n // tn),
        in_specs=[pl.BlockSpec((tm, k), lambda i, j: (i, 0)),
                  pl.BlockSpec((k, tn), lambda i, j: (0, off_u // tn + j)),
                  pl.BlockSpec((k, tn), lambda i, j: (0, off_g // tn + j))],
        out_specs=pl.BlockSpec((tm, tn), lambda i, j: (i, j)),
        out_shape=jax.ShapeDtypeStruct((m, n), F32),
        compiler_params=_params(2), name="glu_proj")(a, w, w)


def _conv_kernel(prev_ref, cur_ref, w_ref, b_ref, g_ref, beta_ref, o_ref, pad_ref, conv_ref, *,
                 zero_first, taps):
    tt, ch = cur_ref.shape[1], cur_ref.shape[2]
    prev_rows = prev_ref.shape[1]
    prev = prev_ref[0]
    if zero_first:
        prev = jnp.where(pl.program_id(1) > 0, prev, 0.0)
    pad_ref[HALO - prev_rows:HALO, :] = prev
    pad_ref[HALO:HALO + tt, :] = cur_ref[0]
    first = HALO - (taps - 1)
    rc, cc = min(32, tt), min(256, ch)
    for r0 in range(0, tt, rc):
        for c0 in range(0, ch, cc):
            acc = jnp.broadcast_to(b_ref[:, c0:c0 + cc], (rc, cc))
            for t in range(taps):
                acc = acc + pad_ref[first + r0 + t:first + r0 + t + rc, c0:c0 + cc] * w_ref[t:t + 1, c0:c0 + cc]
            conv_ref[r0:r0 + rc, c0:c0 + cc] = acc
    lr = min(16, tt)
    g, beta = g_ref[...], beta_ref[...]
    for r0 in range(0, tt, lr):
        y = conv_ref[r0:r0 + lr, :]
        d = y - jnp.mean(y, axis=-1, keepdims=True)
        z = d * lax.rsqrt(jnp.mean(d * d, axis=-1, keepdims=True) + EPS) * g + beta
        o_ref[0, r0:r0 + lr, :] = (z * jax.nn.sigmoid(z)).astype(o_ref.dtype)


def _conv_module(glu, prev, conv_w, conv_b, ln_g, ln_b, out_dtype):
    b, t, ch = glu.shape
    taps = conv_w.shape[0]
    if prev is None:
        tt = _pick(t, 128)
        hb = tt // HALO
        prev_arr = glu
        prev_spec = pl.BlockSpec((1, HALO, ch), lambda bi, i: (bi, jnp.maximum(i * hb - 1, 0), 0))
    else:
        tt = t
        prev_arr = prev
        prev_spec = pl.BlockSpec((1, taps - 1, ch), lambda bi, i: (bi, 0, 0))
    vec = pl.BlockSpec((1, ch), lambda bi, i: (0, 0))
    return pl.pallas_call(
        functools.partial(_conv_kernel, zero_first=prev is None, taps=taps),
        grid=(b, t // tt),
        in_specs=[prev_spec, pl.BlockSpec((1, tt, ch), lambda bi, i: (bi, i, 0)),
                  pl.BlockSpec((taps, ch), lambda bi, i: (0, 0)), vec, vec, vec],
        out_specs=pl.BlockSpec((1, tt, ch), lambda bi, i: (bi, i, 0)),
        out_shape=jax.ShapeDtypeStruct((b, t, ch), out_dtype),
        scratch_shapes=[pltpu.VMEM((HALO + tt, ch), F32), pltpu.VMEM((tt, ch), F32)],
        compiler_params=_params(2), name="conv_module")(
            prev_arr, glu, conv_w, conv_b.reshape(1, ch), ln_g.reshape(1, ch), ln_b.reshape(1, ch))


def _moba_prompt_kernel(q_ref, k_ref, v_ref, o_ref, kmean_ref, *, n_blocks, scale):
    blk = MOBA_BLOCK
    qi = pl.program_id(2)

    @pl.when(qi == 0)
    def _():
        kmean_ref[...] = jnp.zeros_like(kmean_ref)
        for n in range(n_blocks):
            kb = k_ref[n * blk:(n + 1) * blk, :].astype(F32)
            kmean_ref[n:n + 1, :] = jnp.sum(kb, axis=0, keepdims=True) * (1.0 / blk)

    row = lax.broadcasted_iota(jnp.int32, (blk, blk), 0)
    col = lax.broadcasted_iota(jnp.int32, (blk, blk), 1)

    def attend(own):
        q = q_ref[...]
        keep = []
        if own > MOBA_TOPK:
            km = kmean_ref[...]
            km_hi = km.astype(BF16)
            km_lo = (km - km_hi.astype(F32)).astype(BF16)
            gate = _dot_nt(q, km_hi) + _dot_nt(q, km_lo)
            lane = lax.broadcasted_iota(jnp.int32, gate.shape, 1)
            rank = jnp.zeros(gate.shape, jnp.int32)
            for m in range(own):
                gm = gate[:, m:m + 1]
                beats = (gm > gate) | ((gm == gate) & (m < lane))
                rank = rank + jnp.where(beats, 1, 0)
            keep = [rank[:, j:j + 1] < MOBA_TOPK for j in range(own)]
        scores = []
        for j in range(own + 1):
            s = _dot_nt(q, k_ref[j * blk:(j + 1) * blk, :]) * scale
            if j == own:
                s = jnp.where(col <= row, s, -jnp.inf)
            elif keep:
                s = jnp.where(keep[j], s, -jnp.inf)
            scores.append(s)
        m_i = jnp.max(scores[0], axis=1, keepdims=True)
        for s in scores[1:]:
            m_i = jnp.maximum(m_i, jnp.max(s, axis=1, keepdims=True))
        probs = [jnp.exp(s - m_i) for s in scores]
        l_i = jnp.sum(probs[0], axis=1, keepdims=True)
        for p in probs[1:]:
            l_i = l_i + jnp.sum(p, axis=1, keepdims=True)
        p_all = jnp.concatenate([p.astype(BF16) for p in probs], axis=1)
        acc = _dot(p_all, v_ref[0:(own + 1) * blk, :])
        o_ref[...] = (acc / l_i).astype(o_ref.dtype)

    for own in range(n_blocks):
        pl.when(qi == own)(functools.partial(attend, own))


def _moba_prompt(q, k, v, batch, seq, n_heads, head_dim):
    assert seq % MOBA_BLOCK == 0 and seq // MOBA_BLOCK <= LANES
    nb = seq // MOBA_BLOCK
    kv_spec = pl.BlockSpec((seq, head_dim), lambda b, h, i: (b, h))
    q_spec = pl.BlockSpec((MOBA_BLOCK, head_dim), lambda b, h, i: (b * nb + i, h))
    return pl.pallas_call(
        functools.partial(_moba_prompt_kernel, n_blocks=nb, scale=head_dim ** -0.5),
        grid=(batch, n_heads, nb),
        in_specs=[q_spec, kv_spec, kv_spec], out_specs=q_spec,
        out_shape=jax.ShapeDtypeStruct(q.shape, BF16),
        scratch_shapes=[pltpu.VMEM((LANES, head_dim), F32)],
        compiler_params=_params(3), name="moba_prompt")(q, k, v)


def _moba_sample_kernel(pt_ref, q_ref, kn_ref, vn_ref, *rest, n_heads, tq, blocks_per_step, ppb, nbp, scale):
    n_k = blocks_per_step * ppb
    k_refs, v_refs, o_ref = rest[:n_k], rest[n_k:2 * n_k], rest[2 * n_k]
    acc_s, gate_s, m_s, l_s = rest[2 * n_k + 1:]
    g = pl.program_id(1)
    rows, hd = q_ref.shape[1], q_ref.shape[2]
    page = k_refs[0].shape[1] // n_heads
    lane = lax.broadcasted_iota(jnp.int32, (rows, LANES), 1)
    row = lax.broadcasted_iota(jnp.int32, (rows, LANES), 0)
    same_head = (lane % n_heads) == (row % n_heads)
    chunks = page * n_heads // LANES
    q = q_ref[0]
    q_bf = q.astype(BF16)

    @pl.when(g == 0)
    def _():
        gate_s[...] = jnp.zeros_like(gate_s)
        m_s[...] = jnp.zeros_like(m_s)
        l_s[...] = jnp.zeros_like(l_s)

    for j in range(blocks_per_step):
        n = g * blocks_per_step + j
        hit = lane == n
        ksum = jnp.zeros((n_heads, hd), F32)
        scores = []
        for i in range(ppb):
            kp = k_refs[j * ppb + i][0]
            ksum = ksum + jnp.sum(kp.reshape(page, n_heads, hd), axis=0)
            s = _dot_nt(q_bf, kp.astype(BF16)) * scale
            scores += [jnp.where(same_head, s[:, c * LANES:(c + 1) * LANES], -jnp.inf) for c in range(chunks)]
        kmean = ksum * (1.0 / MOBA_BLOCK)
        gate = jnp.sum(q * jnp.concatenate([kmean] * tq, axis=0), axis=1, keepdims=True)
        m_c = scores[0]
        for s in scores[1:]:
            m_c = jnp.maximum(m_c, s)
        m_b = jnp.max(m_c, axis=1, keepdims=True)
        probs = [jnp.exp(s - m_b) for s in scores]
        l_c = probs[0]
        for p in probs[1:]:
            l_c = l_c + p
        acc = jnp.zeros((rows, hd), F32)
        for i in range(ppb):
            p_i = jnp.concatenate([p.astype(BF16) for p in probs[i * chunks:(i + 1) * chunks]], axis=1)
            acc = acc + _dot(p_i, v_refs[j * ppb + i][0].astype(BF16))
        acc_s[n] = acc
        gate_s[...] = jnp.where(hit, gate, gate_s[...])
        m_s[...] = jnp.where(hit, m_b, m_s[...])
        l_s[...] = jnp.where(hit, jnp.sum(l_c, axis=1, keepdims=True), l_s[...])

    @pl.when(g == pl.num_programs(1) - 1)
    def _():
        t_id = lax.broadcasted_iota(jnp.int32, (rows, 1), 0) // n_heads
        kn, vn = kn_ref[0], vn_ref[0]
        kn_c = [jnp.concatenate([kn[c * n_heads:(c + 1) * n_heads]] * tq, axis=0) for c in range(tq)]
        vn_c = [jnp.concatenate([vn[c * n_heads:(c + 1) * n_heads]] * tq, axis=0) for c in range(tq)]
        s_own = [jnp.where(t_id >= c, jnp.sum(q * kn_c[c], axis=1, keepdims=True) * scale, -jnp.inf)
                 for c in range(tq)]
        gt, ms, ls = gate_s[...], m_s[...], l_s[...]
        rank = jnp.zeros(gt.shape, jnp.int32)
        for m in range(nbp):
            gm = gt[:, m:m + 1]
            beats = (gm > gt) | ((gm == gt) & (m < lane))
            rank = rank + jnp.where(beats, 1, 0)
        sel = (lane < nbp) & (rank < MOBA_TOPK)
        m_all = jnp.max(jnp.where(sel, ms, -jnp.inf), axis=1, keepdims=True)
        for s in s_own:
            m_all = jnp.maximum(m_all, s)
        w = jnp.where(sel, jnp.exp(ms - m_all), 0.0)
        den = jnp.sum(w * ls, axis=1, keepdims=True)
        num = jnp.zeros((rows, hd), F32)
        for c, s in enumerate(s_own):
            p = jnp.exp(s - m_all)
            den = den + p
            num = num + p * vn_c[c]
        for n in range(nbp):
            num = num + w[:, n:n + 1] * acc_s[n]
        o_ref[0] = num / den


def _moba_sample(q3, kn3, vn3, cache_k3, cache_v3, pt_flat, n_pages, n_heads, head_dim):
    bs, rows, _ = q3.shape
    tq = rows // n_heads
    page = cache_k3.shape[1] // n_heads
    ppb = MOBA_BLOCK // page
    nbp = n_pages // ppb
    assert MOBA_TOPK <= nbp <= LANES and LANES % n_heads == 0 and rows % 8 == 0
    blocks_per_step = 2 if nbp % 2 == 0 else 1
    n_k = blocks_per_step * ppb
    tok = pl.BlockSpec((1, rows, head_dim), lambda b, g, pt: (b, 0, 0))
    pages = [pl.BlockSpec((1, page * n_heads, head_dim),
                          lambda b, g, pt, i=i: (pt[b * n_pages + g * n_k + i], 0, 0)) for i in range(n_k)]
    stat = pltpu.VMEM((rows, LANES), F32)
    return pl.pallas_call(
        functools.partial(_moba_sample_kernel, n_heads=n_heads, tq=tq, blocks_per_step=blocks_per_step,
                          ppb=ppb, nbp=nbp, scale=head_dim ** -0.5),
        grid_spec=pltpu.PrefetchScalarGridSpec(
            num_scalar_prefetch=1, grid=(bs, nbp // blocks_per_step),
            in_specs=[tok, tok, tok] + pages + pages, out_specs=tok,
            scratch_shapes=[pltpu.VMEM((nbp, rows, head_dim), F32), stat, stat, stat]),
        out_shape=jax.ShapeDtypeStruct((bs, rows, head_dim), F32),
        compiler_params=_params(2), name="moba_sample")(
            pt_flat, q3, kn3, vn3, *([cache_k3] * n_k), *([cache_v3] * n_k))


def _merge_kernel(h_ref, at_ref, cv_ref, wga_ref, wgc_ref, wap_ref, wcp_ref, o_ref):
    h = h_ref[...]
    ga = jax.nn.sigmoid(_dot(h, wga_ref[...]))
    gc = jax.nn.sigmoid(_dot(h, wgc_ref[...]))
    o_ref[...] = (ga * _dot(at_ref[...], wap_ref[...]) + gc * _dot(cv_ref[...], wcp_ref[...])).astype(o_ref.dtype)


def _merge(h, attn, cv, w_in, off_ga, off_gc, w_ap, w_cp):
    m, d = h.shape
    n = w_ap.shape[1]
    tm, tn = _pick(m, 512), _pick(n, 512)
    return pl.pallas_call(
        _merge_kernel, grid=(m // tm, n // tn),
        in_specs=[pl.BlockSpec((tm, d), lambda i, j: (i, 0)),
                  pl.BlockSpec((tm, attn.shape[1]), lambda i, j: (i, 0)),
                  pl.BlockSpec((tm, cv.shape[1]), lambda i, j: (i, 0)),
                  pl.BlockSpec((d, tn), lambda i, j: (0, off_ga // tn + j)),
                  pl.BlockSpec((d, tn), lambda i, j: (0, off_gc // tn + j)),
                  pl.BlockSpec((w_ap.shape[0], tn), lambda i, j: (0, j)),
                  pl.BlockSpec((w_cp.shape[0], tn), lambda i, j: (0, j))],
        out_specs=pl.BlockSpec((tm, tn), lambda i, j: (i, j)),
        out_shape=jax.ShapeDtypeStruct((m, n), BF16),
        compiler_params=_params(2), name="gated_merge")(h, attn, cv, w_in, w_in, w_ap, w_cp)


def _resid_kernel(a_ref, w_ref, x_ref, g_ref, o_ref):
    o_ref[...] = x_ref[...] + _row(g_ref) * _dot(a_ref[...], w_ref[...])


def _resid_proj(a, w, x, mod, which):
    m, k = a.shape
    n = w.shape[1]
    tm, tn = _pick(mod.rpb, 1024), _pick(n, 512)
    return pl.pallas_call(
        _resid_kernel, grid=(m // tm, n // tn),
        in_specs=[pl.BlockSpec((tm, k), lambda i, j: (i, 0)),
                  pl.BlockSpec((k, tn), lambda i, j: (0, j)),
                  pl.BlockSpec((tm, tn), lambda i, j: (i, j)),
                  mod.spec(which, tm, tn)],
        out_specs=pl.BlockSpec((tm, tn), lambda i, j: (i, j)),
        out_shape=jax.ShapeDtypeStruct((m, n), F32),
        compiler_params=_params(2), name="out_proj")(a, w, x, mod.arr)


def _ff1_kernel(a_ref, w_ref, o_ref):
    r = jnp.maximum(_dot(a_ref[...], w_ref[...]), 0.0)
    o_ref[...] = (r * r).astype(o_ref.dtype)


def _ff1(a, w):
    m, k = a.shape
    n = w.shape[1]
    tm, tn = _pick(m, 1024), _pick(n, 1024)
    return pl.pallas_call(
        _ff1_kernel, grid=(m // tm, n // tn),
        in_specs=[pl.BlockSpec((tm, k), lambda i, j: (i, 0)), pl.BlockSpec((k, tn), lambda i, j: (0, j))],
        out_specs=pl.BlockSpec((tm, tn), lambda i, j: (i, j)),
        out_shape=jax.ShapeDtypeStruct((m, n), BF16),
        compiler_params=_params(2), name="ff1")(a, w)


def _ff2_kernel(a_ref, w_ref, x_ref, g_ref, o_ref, acc_ref):
    kk = pl.program_id(2)

    @pl.when(kk == 0)
    def _():
        acc_ref[...] = jnp.zeros_like(acc_ref)

    acc_ref[...] += _dot(a_ref[...], w_ref[...])

    @pl.when(kk == pl.num_programs(2) - 1)
    def _():
        o_ref[...] = x_ref[...] + _row(g_ref) * acc_ref[...]


def _ff2(a, w, x, mod, which):
    m, k = a.shape
    n = w.shape[1]
    tm, tn, tk = _pick(mod.rpb, 1024), _pick(n, 1024), _pick(k, 2048)
    return pl.pallas_call(
        _ff2_kernel, grid=(m // tm, n // tn, k // tk),
        in_specs=[pl.BlockSpec((tm, tk), lambda i, j, kk: (i, kk)),
                  pl.BlockSpec((tk, tn), lambda i, j, kk: (kk, j)),
                  pl.BlockSpec((tm, tn), lambda i, j, kk: (i, j)),
                  mod.spec(which, tm, tn)],
        out_specs=pl.BlockSpec((tm, tn), lambda i, j, kk: (i, j)),
        out_shape=jax.ShapeDtypeStruct((m, n), F32),
        scratch_shapes=[pltpu.VMEM((tm, tn), F32)],
        compiler_params=_params(3), name="ff2")(a, w, x, mod.arr)


def _rope_tables(pos, head_dim):
    half = head_dim // 2
    inv_freq = ROPE_THETA ** (-jnp.arange(half, dtype=F32) / half)
    ang = pos.astype(F32)[:, None] * inv_freq[None, :]
    cos, sin = jnp.cos(ang), jnp.sin(ang)
    return jnp.concatenate([cos, cos], axis=1), jnp.concatenate([-sin, sin], axis=1)


def _layer(x, mod, pos_rows, group, lw, n_heads, head_dim, d_conv):
    (norm1_g, w_in, conv_w, conv_b, ln_g, ln_b, w_ap, w_cp, w_out, norm2_g, w_ff1, w_ff2) = lw
    m, d = x.shape
    d_attn = n_heads * head_dim
    off_k, off_v, off_u = d_attn, 2 * d_attn, 3 * d_attn
    off_ug, off_ga = off_u + d_conv, off_u + 2 * d_conv
    off_gc = off_ga + d
    tabs = _rope_tables(pos_rows, head_dim)

    h = _norm_mod(x, norm1_g, mod, 1, 0)
    glu = _glu_proj(h, w_in, off_u, off_ug, d_conv)
    if group[0] == "prompt":
        _, batch, seq = group
        (q_b,) = _proj(h, w_in, 0, d_attn, [BF16], seq, tabs, head_dim)
        k_f, k_b = _proj(h, w_in, off_k, d_attn, [F32, BF16], seq, tabs, head_dim)
        v_f, v_b = _proj(h, w_in, off_v, d_attn, [F32, BF16], seq)
        attn = _moba_prompt(q_b, k_b, v_b, batch, seq, n_heads, head_dim)
        glu3 = glu.reshape(batch, seq, d_conv)
        cv = _conv_module(glu3, None, conv_w, conv_b, ln_g, ln_b, BF16).reshape(m, d_conv)
        conv_state = glu3[:, seq - (conv_w.shape[0] - 1):]
    else:
        _, cache_k3, cache_v3, state, page_table = group
        bs, n_pages = page_table.shape
        tq = m // bs
        (q_f,) = _proj(h, w_in, 0, d_attn, [F32], m, tabs, head_dim)
        (k_f,) = _proj(h, w_in, off_k, d_attn, [F32], m, tabs, head_dim)
        (v_f,) = _proj(h, w_in, off_v, d_attn, [F32], m)
        tok_rows = lambda a: a.reshape(bs, tq * n_heads, head_dim)
        attn = _moba_sample(tok_rows(q_f), tok_rows(k_f), tok_rows(v_f), cache_k3, cache_v3,
                            page_table.reshape(-1), n_pages, n_heads, head_dim)
        attn = attn.reshape(m, d_attn).astype(BF16)
        glu3 = glu.reshape(bs, tq, d_conv)
        cv = _conv_module(glu3, state, conv_w, conv_b, ln_g, ln_b, F32).reshape(m, d_conv).astype(BF16)
        conv_state = jnp.concatenate([state, glu3], axis=1)[:, tq:]

    mixed = _merge(h, attn, cv, w_in, off_ga, off_gc, w_ap, w_cp)
    x1 = _resid_proj(mixed, w_out, x, mod, 2)
    h2 = _norm_mod(x1, norm2_g, mod, 4, 3)
    x2 = _ff2(_ff1(h2, w_ff1), w_ff2, x1, mod, 5)
    return x2, k_f, v_f, conv_state


def kernel(x_prompt, x_sample, cache_k, cache_v, state_conv, page_table, c_prompt, c_sample, ada_w, ada_b,
           norm1_g, w_in, conv_w, conv_b, conv_ln_g, conv_ln_b, w_attn_proj, w_conv_proj, w_out, norm2_g,
           w_ff1, w_ff2, final_g):
    bp, seq, d = x_prompt.shape
    bs, tq, _ = x_sample.shape
    depth, n_pool, page, n_heads, head_dim = cache_k.shape
    d_conv = state_conv.shape[-1]
    n_pages = page_table.shape[1]
    assert MOBA_BLOCK % page == 0 and (n_pages * page) % MOBA_BLOCK == 0 and tq <= MOBA_BLOCK

    yp = x_prompt.reshape(bp * seq, d)
    ys = x_sample.reshape(bs * tq, d)
    n_c = bp + bs
    c_rows = -(-n_c // 8) * 8
    c_all = jnp.concatenate([c_prompt, c_sample, jnp.zeros((c_rows - n_c, d), F32)], axis=0)
    pos_p = jnp.arange(seq, dtype=jnp.int32)
    pos_s = n_pages * page + jnp.tile(jnp.arange(tq, dtype=jnp.int32), bs)

    outs = [[] for _ in range(6)]
    for l in range(depth):
        mod_all = _ada(c_all, ada_w[l], ada_b[l])
        mod_p = _Mod(mod_all[:bp].reshape(bp * 6, 1, d), False, d, seq)
        mod_s = _Mod(jnp.repeat(mod_all[bp:n_c], tq, axis=0), True, d, bs * tq)
        lw = (norm1_g[l], w_in[l].astype(BF16), conv_w[l], conv_b[l], conv_ln_g[l], conv_ln_b[l],
              w_attn_proj[l].astype(BF16), w_conv_proj[l].astype(BF16), w_out[l].astype(BF16), norm2_g[l],
              w_ff1[l].astype(BF16), w_ff2[l].astype(BF16))
        yp, kp, vp, cp = _layer(yp, mod_p, pos_p, ("prompt", bp, seq), lw, n_heads, head_dim, d_conv)
        group_s = ("sample", cache_k[l].reshape(n_pool, page * n_heads, head_dim),
                   cache_v[l].reshape(n_pool, page * n_heads, head_dim), state_conv[l], page_table)
        ys, kn, vn, cn = _layer(ys, mod_s, pos_s, group_s, lw, n_heads, head_dim, d_conv)
        for lst, val in zip(outs, (kp.reshape(bp, seq, n_heads, head_dim), vp.reshape(bp, seq, n_heads, head_dim), cp,
                                   kn.reshape(bs, tq, n_heads, head_dim), vn.reshape(bs, tq, n_heads, head_dim), cn)):
            lst.append(val)
    y_prompt = _final_norm(yp, final_g).reshape(bp, seq, d)
    y_sample = _final_norm(ys, final_g).reshape(bs, tq, d)
    return (y_prompt, y_sample) + tuple(jnp.stack(o) for o in outs)
```

```python
import functools

import jax
import jax.numpy as jnp
from jax import lax
from jax.experimental import pallas as pl
from jax.experimental.pallas import tpu as pltpu

MOBA_BLOCK = 256
MOBA_TOPK = 3
ROPE_THETA = 10000.0
EPS = 1e-6
LANES = 128
SUBLANES = 8
HALO = 32
VMEM_LIMIT = 56 * 1024 * 1024

F32 = jnp.float32
BF16 = jnp.bfloat16


def _pick(n, pref):
    if n <= pref:
        return n
    t = pref
    while n % t:
        t //= 2
    return t


def _params(n_axes):
    return pltpu.CompilerParams(dimension_semantics=("arbitrary",) * n_axes,
                                vmem_limit_bytes=VMEM_LIMIT)


def _dot(a, b):
    return jnp.dot(a, b, preferred_element_type=F32)


def _dot_nt(a, b):
    return lax.dot_general(a, b, (((1,), (1,)), ((), ())), preferred_element_type=F32)


def _row(ref):
    v = ref[...]
    return v.reshape(v.shape[-2:])


class _Mod:
    def __init__(self, arr, per_row, d_model, rows_per_batch):
        self.arr, self.per_row, self.d, self.rpb = arr, per_row, d_model, rows_per_batch

    def spec(self, which, tm, tn):
        if self.per_row:
            nj = self.d // tn
            return pl.BlockSpec((tm, tn), lambda i, j, *_: (i, which * nj + j))
        tpb = self.rpb // tm
        return pl.BlockSpec((1, 1, tn), lambda i, j, *_: ((i // tpb) * 6 + which, 0, j))


def _ada_kernel(c_ref, w_ref, b_ref, o_ref):
    o_ref[...] = _dot(c_ref[...].astype(BF16), w_ref[...].astype(BF16)) + b_ref[...]


def _ada(c, w, b):
    r, d = c.shape
    n = w.shape[1]
    tn = _pick(n, 512)
    return pl.pallas_call(
        _ada_kernel, grid=(n // tn,),
        in_specs=[pl.BlockSpec((r, d), lambda j: (0, 0)),
                  pl.BlockSpec((d, tn), lambda j: (0, j)),
                  pl.BlockSpec((1, tn), lambda j: (0, j))],
        out_specs=pl.BlockSpec((r, tn), lambda j: (0, j)),
        out_shape=jax.ShapeDtypeStruct((r, n), F32),
        compiler_params=_params(1), name="ada_mod")(c, w, b.reshape(1, n))


def _norm_mod_kernel(x_ref, g_ref, sc_ref, sh_ref, o_ref):
    x = x_ref[...]
    y = x * lax.rsqrt(jnp.mean(x * x, axis=-1, keepdims=True) + EPS) * g_ref[...]
    o_ref[...] = (y * (1.0 + _row(sc_ref)) + _row(sh_ref)).astype(o_ref.dtype)


def _norm_mod(x, g, mod, which_sc, which_sh):
    m, d = x.shape
    tm = _pick(mod.rpb, 256)
    return pl.pallas_call(
        _norm_mod_kernel, grid=(m // tm, 1),
        in_specs=[pl.BlockSpec((tm, d), lambda i, j: (i, 0)),
                  pl.BlockSpec((1, d), lambda i, j: (0, 0)),
                  mod.spec(which_sc, tm, d), mod.spec(which_sh, tm, d)],
        out_specs=pl.BlockSpec((tm, d), lambda i, j: (i, 0)),
        out_shape=jax.ShapeDtypeStruct((m, d), BF16),
        compiler_params=_params(2), name="norm_mod")(x, g.reshape(1, d), mod.arr, mod.arr)


def _final_norm_kernel(x_ref, g_ref, o_ref):
    x = x_ref[...]
    o_ref[...] = x * lax.rsqrt(jnp.mean(x * x, axis=-1, keepdims=True) + EPS) * g_ref[...]


def _final_norm(x, g):
    m, d = x.shape
    tm = _pick(m, 256)
    return pl.pallas_call(
        _final_norm_kernel, grid=(m // tm,),
        in_specs=[pl.BlockSpec((tm, d), lambda i: (i, 0)), pl.BlockSpec((1, d), lambda i: (0, 0))],
        out_specs=pl.BlockSpec((tm, d), lambda i: (i, 0)),
        out_shape=jax.ShapeDtypeStruct((m, d), F32),
        compiler_params=_params(1), name="final_norm")(x, g.reshape(1, d))


def _proj_kernel(*refs, rope, head_dim):
    a_ref, w_ref = refs[:2]
    outs = refs[4:] if rope else refs[2:]
    acc = _dot(a_ref[...], w_ref[...])
    if not rope:
        for o in outs:
            o[...] = acc.astype(o.dtype)
        return
    cos, sin = refs[2][...], refs[3][...]
    for c in range(acc.shape[1] // head_dim):
        xh = acc[:, c * head_dim:(c + 1) * head_dim]
        r = xh * cos + pltpu.roll(xh, head_dim // 2, 1) * sin
        for o in outs:
            o[:, c * head_dim:(c + 1) * head_dim] = r.astype(o.dtype)


def _proj(a, w, col_off, n, out_dtypes, rpb, rope_tabs=None, head_dim=LANES):
    m, k = a.shape
    tm, tn = _pick(rpb, 1024), _pick(n, 512)
    in_specs = [pl.BlockSpec((tm, k), lambda i, j: (i, 0)),
                pl.BlockSpec((k, tn), lambda i, j: (0, col_off // tn + j))]
    args = [a, w]
    if rope_tabs is not None:
        tpb = rope_tabs[0].shape[0] // tm
        tab_spec = pl.BlockSpec((tm, head_dim), lambda i, j: (i % tpb, 0))
        in_specs += [tab_spec, tab_spec]
        args += list(rope_tabs)
    return pl.pallas_call(
        functools.partial(_proj_kernel, rope=rope_tabs is not None, head_dim=head_dim),
        grid=(m // tm, n // tn), in_specs=in_specs,
        out_specs=[pl.BlockSpec((tm, tn), lambda i, j: (i, j)) for _ in out_dtypes],
        out_shape=[jax.ShapeDtypeStruct((m, n), dt) for dt in out_dtypes],
        compiler_params=_params(2), name="in_proj")(*args)


def _glu_kernel(a_ref, wu_ref, wg_ref, o_ref):
    a = a_ref[...]
    o_ref[...] = _dot(a, wu_ref[...]) * jax.nn.sigmoid(_dot(a, wg_ref[...]))


def _glu_proj(a, w, off_u, off_g, n):
    m, k = a.shape
    tm, tn = _pick(m, 1024), _pick(n, 512)
    return pl.pallas_call(
        _glu_kernel, grid=(m // tm, n // tn),
        in_specs=[pl.BlockSpec((tm, k), lambda i, j: (i, 0)),
                  pl.BlockSpec((k, tn), lambda i, j: (0, off_u // tn + j)),
                  pl.BlockSpec((k, tn), lambda i, j: (0, off_g // tn + j))],
        out_specs=pl.BlockSpec((tm, tn), lambda i, j: (i, j)),
        out_shape=jax.ShapeDtypeStruct((m, n), F32),
        compiler_params=_params(2), name="glu_proj")(a, w, w)


def _conv_kernel(prev_ref, cur_ref, w_ref, b_ref, g_ref, beta_ref, o_ref, pad_ref, conv_ref, *shift_refs,
                 zero_first, taps):
    tt, ch = cur_ref.shape[1], cur_ref.shape[2]
    prev_rows = prev_ref.shape[1]
    prev = prev_ref[0]
    if zero_first:
        prev = jnp.where(pl.program_id(1) > 0, prev, 0.0)
    pad_ref[HALO - prev_rows:HALO, :] = prev
    pad_ref[HALO:HALO + tt, :] = cur_ref[0]
    first = HALO - (taps - 1)
    rc, cc = min(32, tt), min(256, ch)
    if shift_refs:
        (shift_ref,) = shift_refs
        n_sh = shift_ref.shape[1]
        for s in range(1, SUBLANES):
            for c0 in range(0, ch, cc):
                shift_ref[s - 1, :, c0:c0 + cc] = pad_ref[s:s + n_sh, c0:c0 + cc]

    def tap_rows(r0, t, c0):
        a, s = divmod(first + t, SUBLANES)
        if not shift_refs:
            return pad_ref[first + r0 + t:first + r0 + t + rc, c0:c0 + cc]
        src = pad_ref if s == 0 else shift_refs[0].at[s - 1]
        return src[r0 + a * SUBLANES:r0 + a * SUBLANES + rc, c0:c0 + cc]

    for r0 in range(0, tt, rc):
        for c0 in range(0, ch, cc):
            acc = jnp.broadcast_to(b_ref[:, c0:c0 + cc], (rc, cc))
            for t in range(taps):
                acc = acc + tap_rows(r0, t, c0) * w_ref[t:t + 1, c0:c0 + cc]
            conv_ref[r0:r0 + rc, c0:c0 + cc] = acc
    lr = min(16, tt)
    g, beta = g_ref[...], beta_ref[...]
    for r0 in range(0, tt, lr):
        y = conv_ref[r0:r0 + lr, :]
        d = y - jnp.mean(y, axis=-1, keepdims=True)
        z = d * lax.rsqrt(jnp.mean(d * d, axis=-1, keepdims=True) + EPS) * g + beta
        o_ref[0, r0:r0 + lr, :] = (z * jax.nn.sigmoid(z)).astype(o_ref.dtype)


def _conv_module(glu, prev, conv_w, conv_b, ln_g, ln_b, out_dtype):
    b, t, ch = glu.shape
    taps = conv_w.shape[0]
    if prev is None:
        tt = _pick(t, 128)
        hb = tt // HALO
        prev_arr = glu
        prev_spec = pl.BlockSpec((1, HALO, ch), lambda bi, i: (bi, jnp.maximum(i * hb - 1, 0), 0))
    else:
        tt = t
        prev_arr = prev
        prev_spec = pl.BlockSpec((1, taps - 1, ch), lambda bi, i: (bi, 0, 0))
    vec = pl.BlockSpec((1, ch), lambda bi, i: (0, 0))
    scratch = [pltpu.VMEM((HALO + tt, ch), F32), pltpu.VMEM((tt, ch), F32)]
    if prev is None and tt % SUBLANES == 0:
        scratch.append(pltpu.VMEM((SUBLANES - 1, HALO + tt - SUBLANES, ch), F32))
    return pl.pallas_call(
        functools.partial(_conv_kernel, zero_first=prev is None, taps=taps),
        grid=(b, t // tt),
        in_specs=[prev_spec, pl.BlockSpec((1, tt, ch), lambda bi, i: (bi, i, 0)),
                  pl.BlockSpec((taps, ch), lambda bi, i: (0, 0)), vec, vec, vec],
        out_specs=pl.BlockSpec((1, tt, ch), lambda bi, i: (bi, i, 0)),
        out_shape=jax.ShapeDtypeStruct((b, t, ch), out_dtype),
        scratch_shapes=scratch,
        compiler_params=_params(2), name="conv_module")(
            prev_arr, glu, conv_w, conv_b.reshape(1, ch), ln_g.reshape(1, ch), ln_b.reshape(1, ch))


def _moba_prompt_kernel(q_ref, k_ref, v_ref, o_ref, kmean_ref, *, n_blocks, scale):
    blk = MOBA_BLOCK
    kmean_ref[...] = jnp.zeros_like(kmean_ref)
    for n in range(n_blocks):
        kb = k_ref[n * blk:(n + 1) * blk, :].astype(F32)
        kmean_ref[n:n + 1, :] = jnp.sum(kb, axis=0, keepdims=True) * (1.0 / blk)
    km = kmean_ref[...]
    km_hi = km.astype(BF16)
    km_lo = (km - km_hi.astype(F32)).astype(BF16)
    row = lax.broadcasted_iota(jnp.int32, (blk, blk), 0)
    col = lax.broadcasted_iota(jnp.int32, (blk, blk), 1)
    lane = lax.broadcasted_iota(jnp.int32, (blk, LANES), 1)

    def attend(own):
        q = q_ref[own * blk:(own + 1) * blk, :]
        keep = []
        if own > MOBA_TOPK:
            gate = _dot_nt(q, km_hi) + _dot_nt(q, km_lo)
            rank = jnp.zeros(gate.shape, jnp.int32)
            for m in range(own):
                gm = gate[:, m:m + 1]
                beats = (gm > gate) | ((gm == gate) & (m < lane))
                rank = rank + jnp.where(beats, 1, 0)
            keep = [rank[:, j:j + 1] < MOBA_TOPK for j in range(own)]
        scores = []
        for j in range(own + 1):
            s = _dot_nt(q, k_ref[j * blk:(j + 1) * blk, :]) * scale
            if j == own:
                s = jnp.where(col <= row, s, -jnp.inf)
            elif keep:
                s = jnp.where(keep[j], s, -jnp.inf)
            scores.append(s)
        m_i = jnp.max(scores[0], axis=1, keepdims=True)
        for s in scores[1:]:
            m_i = jnp.maximum(m_i, jnp.max(s, axis=1, keepdims=True))
        probs = [jnp.exp(s - m_i) for s in scores]
        l_i = jnp.sum(probs[0], axis=1, keepdims=True)
        for p in probs[1:]:
            l_i = l_i + jnp.sum(p, axis=1, keepdims=True)
        p_all = jnp.concatenate([p.astype(BF16) for p in probs], axis=1)
        acc = _dot(p_all, v_ref[0:(own + 1) * blk, :])
        o_ref[own * blk:(own + 1) * blk, :] = (acc / l_i).astype(o_ref.dtype)

    for own in range(n_blocks):
        attend(own)


def _moba_prompt(q, k, v, batch, seq, n_heads, head_dim):
    assert seq % MOBA_BLOCK == 0 and seq // MOBA_BLOCK <= LANES
    spec = pl.BlockSpec((seq, head_dim), lambda b, h: (b, h))
    return pl.pallas_call(
        functools.partial(_moba_prompt_kernel, n_blocks=seq // MOBA_BLOCK, scale=head_dim ** -0.5),
        grid=(batch, n_heads),
        in_specs=[spec, spec, spec], out_specs=spec,
        out_shape=jax.ShapeDtypeStruct(q.shape, BF16),
        scratch_shapes=[pltpu.VMEM((LANES, head_dim), F32)],
        compiler_params=_params(2), name="moba_prompt")(q, k, v)


def _moba_sample_kernel(pt_ref, q_ref, kn_ref, vn_ref, *rest, n_heads, tq, blocks_per_step, ppb, nbp, scale):
    n_k = blocks_per_step * ppb
    k_refs, v_refs, o_ref = rest[:n_k], rest[n_k:2 * n_k], rest[2 * n_k]
    acc_s, gate_s, m_s, l_s = rest[2 * n_k + 1:]
    g = pl.program_id(1)
    rows, hd = q_ref.shape[1], q_ref.shape[2]
    page = k_refs[0].shape[1] // n_heads
    lane = lax.broadcasted_iota(jnp.int32, (rows, LANES), 1)
    row = lax.broadcasted_iota(jnp.int32, (rows, LANES), 0)
    same_head = (lane % n_heads) == (row % n_heads)
    chunks = page * n_heads // LANES
    q = q_ref[0]
    q_bf = q.astype(BF16)

    @pl.when(g == 0)
    def _():
        gate_s[...] = jnp.zeros_like(gate_s)
        m_s[...] = jnp.zeros_like(m_s)
        l_s[...] = jnp.zeros_like(l_s)

    for j in range(blocks_per_step):
        n = g * blocks_per_step + j
        hit = lane == n
        ksum = jnp.zeros((n_heads, hd), F32)
        scores = []
        for i in range(ppb):
            kp = k_refs[j * ppb + i][0]
            ksum = ksum + jnp.sum(kp.reshape(page, n_heads, hd), axis=0)
            s = _dot_nt(q_bf, kp.astype(BF16)) * scale
            scores += [jnp.where(same_head, s[:, c * LANES:(c + 1) * LANES], -jnp.inf) for c in range(chunks)]
        kmean = ksum * (1.0 / MOBA_BLOCK)
        gate = jnp.sum(q * jnp.concatenate([kmean] * tq, axis=0), axis=1, keepdims=True)
        m_c = scores[0]
        for s in scores[1:]:
            m_c = jnp.maximum(m_c, s)
        m_b = jnp.max(m_c, axis=1, keepdims=True)
        probs = [jnp.exp(s - m_b) for s in scores]
        l_c = probs[0]
        for p in probs[1:]:
            l_c = l_c + p
        acc = jnp.zeros((rows, hd), F32)
        for i in range(ppb):
            p_i = jnp.concatenate([p.astype(BF16) for p in probs[i * chunks:(i + 1) * chunks]], axis=1)
            acc = acc + _dot(p_i, v_refs[j * ppb + i][0].astype(BF16))
        acc_s[n] = acc
        gate_s[...] = jnp.where(hit, gate, gate_s[...])
        m_s[...] = jnp.where(hit, m_b, m_s[...])
        l_s[...] = jnp.where(hit, jnp.sum(l_c, axis=1, keepdims=True), l_s[...])

    @pl.when(g == pl.num_programs(1) - 1)
    def _():
        t_id = lax.broadcasted_iota(jnp.int32, (rows, 1), 0) // n_heads
        kn, vn = kn_ref[0], vn_ref[0]
        kn_c = [jnp.concatenate([kn[c * n_heads:(c + 1) * n_heads]] * tq, axis=0) for c in range(tq)]
        vn_c = [jnp.concatenate([vn[c * n_heads:(c + 1) * n_heads]] * tq, axis=0) for c in range(tq)]
        s_own = [jnp.where(t_id >= c, jnp.sum(q * kn_c[c], axis=1, keepdims=True) * scale, -jnp.inf)
                 for c in range(tq)]
        gt, ms, ls = gate_s[...], m_s[...], l_s[...]
        rank = jnp.zeros(gt.shape, jnp.int32)
        for m in range(nbp):
            gm = gt[:, m:m + 1]
            beats = (gm > gt) | ((gm == gt) & (m < lane))
            rank = rank + jnp.where(beats, 1, 0)
        sel = (lane < nbp) & (rank < MOBA_TOPK)
        m_all = jnp.max(jnp.where(sel, ms, -jnp.inf), axis=1, keepdims=True)
        for s in s_own:
            m_all = jnp.maximum(m_all, s)
        w = jnp.where(sel, jnp.exp(ms - m_all), 0.0)
        den = jnp.sum(w * ls, axis=1, keepdims=True)
        num = jnp.zeros((rows, hd), F32)
        for c, s in enumerate(s_own):
            p = jnp.exp(s - m_all)
            den = den + p
            num = num + p * vn_c[c]
        for n in range(nbp):
            num = num + w[:, n:n + 1] * acc_s[n]
        o_ref[0] = num / den


def _moba_sample(q3, kn3, vn3, cache_k3, cache_v3, pt_flat, n_pages, n_heads, head_dim):
    bs, rows, _ = q3.shape
    tq = rows // n_heads
    page = cache_k3.shape[1] // n_heads
    ppb = MOBA_BLOCK // page
    nbp = n_pages // ppb
    assert MOBA_TOPK <= nbp <= LANES and LANES % n_heads == 0 and rows % 8 == 0
    blocks_per_step = next(c for c in (4, 2, 1) if nbp % c == 0)
    n_k = blocks_per_step * ppb
    tok = pl.BlockSpec((1, rows, head_dim), lambda b, g, pt: (b, 0, 0))
    pages = [pl.BlockSpec((1, page * n_heads, head_dim),
                          lambda b, g, pt, i=i: (pt[b * n_pages + g * n_k + i], 0, 0)) for i in range(n_k)]
    stat = pltpu.VMEM((rows, LANES), F32)
    return pl.pallas_call(
        functools.partial(_moba_sample_kernel, n_heads=n_heads, tq=tq, blocks_per_step=blocks_per_step,
                          ppb=ppb, nbp=nbp, scale=head_dim ** -0.5),
        grid_spec=pltpu.PrefetchScalarGridSpec(
            num_scalar_prefetch=1, grid=(bs, nbp // blocks_per_step),
            in_specs=[tok, tok, tok] + pages + pages, out_specs=tok,
            scratch_shapes=[pltpu.VMEM((nbp, rows, head_dim), F32), stat, stat, stat]),
        out_shape=jax.ShapeDtypeStruct((bs, rows, head_dim), F32),
        compiler_params=_params(2), name="moba_sample")(
            pt_flat, q3, kn3, vn3, *([cache_k3] * n_k), *([cache_v3] * n_k))


def _merge_kernel(h_ref, at_ref, cv_ref, wga_ref, wgc_ref, wap_ref, wcp_ref, o_ref):
    h = h_ref[...]
    ga = jax.nn.sigmoid(_dot(h, wga_ref[...]))
    gc = jax.nn.sigmoid(_dot(h, wgc_ref[...]))
    o_ref[...] = (ga * _dot(at_ref[...], wap_ref[...]) + gc * _dot(cv_ref[...], wcp_ref[...])).astype(o_ref.dtype)


def _merge(h, attn, cv, w_in, off_ga, off_gc, w_ap, w_cp):
    m, d = h.shape
    n = w_ap.shape[1]
    tm, tn = _pick(m, 512), _pick(n, 512)
    return pl.pallas_call(
        _merge_kernel, grid=(m // tm, n // tn),
        in_specs=[pl.BlockSpec((tm, d), lambda i, j: (i, 0)),
                  pl.BlockSpec((tm, attn.shape[1]), lambda i, j: (i, 0)),
                  pl.BlockSpec((tm, cv.shape[1]), lambda i, j: (i, 0)),
                  pl.BlockSpec((d, tn), lambda i, j: (0, off_ga // tn + j)),
                  pl.BlockSpec((d, tn), lambda i, j: (0, off_gc // tn + j)),
                  pl.BlockSpec((w_ap.shape[0], tn), lambda i, j: (0, j)),
                  pl.BlockSpec((w_cp.shape[0], tn), lambda i, j: (0, j))],
        out_specs=pl.BlockSpec((tm, tn), lambda i, j: (i, j)),
        out_shape=jax.ShapeDtypeStruct((m, n), BF16),
        compiler_params=_params(2), name="gated_merge")(h, attn, cv, w_in, w_in, w_ap, w_cp)


def _resid_kernel(a_ref, w_ref, x_ref, g_ref, o_ref):
    o_ref[...] = x_ref[...] + _row(g_ref) * _dot(a_ref[...], w_ref[...])


def _resid_proj(a, w, x, mod, which):
    m, k = a.shape
    n = w.shape[1]
    tm, tn = _pick(mod.rpb, 1024), _pick(n, 512)
    return pl.pallas_call(
        _resid_kernel, grid=(m // tm, n // tn),
        in_specs=[pl.BlockSpec((tm, k), lambda i, j: (i, 0)),
                  pl.BlockSpec((k, tn), lambda i, j: (0, j)),
                  pl.BlockSpec((tm, tn), lambda i, j: (i, j)),
                  mod.spec(which, tm, tn)],
        out_specs=pl.BlockSpec((tm, tn), lambda i, j: (i, j)),
        out_shape=jax.ShapeDtypeStruct((m, n), F32),
        compiler_params=_params(2), name="out_proj")(a, w, x, mod.arr)


def _ff1_kernel(a_ref, w_ref, o_ref):
    r = jnp.maximum(_dot(a_ref[...], w_ref[...]), 0.0)
    o_ref[...] = (r * r).astype(o_ref.dtype)


def _ff1(a, w):
    m, k = a.shape
    n = w.shape[1]
    tm, tn = _pick(m, 1024), _pick(n, 1024)
    return pl.pallas_call(
        _ff1_kernel, grid=(m // tm, n // tn),
        in_specs=[pl.BlockSpec((tm, k), lambda i, j: (i, 0)), pl.BlockSpec((k, tn), lambda i, j: (0, j))],
        out_specs=pl.BlockSpec((tm, tn), lambda i, j: (i, j)),
        out_shape=jax.ShapeDtypeStruct((m, n), BF16),
        compiler_params=_params(2), name="ff1")(a, w)


def _ff2_kernel(a_ref, w_ref, x_ref, g_ref, o_ref, acc_ref):
    kk = pl.program_id(2)

    @pl.when(kk == 0)
    def _():
        acc_ref[...] = jnp.zeros_like(acc_ref)

    acc_ref[...] += _dot(a_ref[...], w_ref[...])

    @pl.when(kk == pl.num_programs(2) - 1)
    def _():
        o_ref[...] = x_ref[...] + _row(g_ref) * acc_ref[...]


def _ff2(a, w, x, mod, which):
    m, k = a.shape
    n = w.shape[1]
    tm, tn, tk = _pick(mod.rpb, 1024), _pick(n, 1024), _pick(k, 2048)
    return pl.pallas_call(
        _ff2_kernel, grid=(m // tm, n // tn, k // tk),
        in_specs=[pl.BlockSpec((tm, tk), lambda i, j, kk: (i, kk)),
                  pl.BlockSpec((tk, tn), lambda i, j, kk: (kk, j)),
                  pl.BlockSpec((tm, tn), lambda i, j, kk: (i, j)),
                  mod.spec(which, tm, tn)],
        out_specs=pl.BlockSpec((tm, tn), lambda i, j, kk: (i, j)),
        out_shape=jax.ShapeDtypeStruct((m, n), F32),
        scratch_shapes=[pltpu.VMEM((tm, tn), F32)],
        compiler_params=_params(3), name="ff2")(a, w, x, mod.arr)


def _rope_tables(pos, head_dim):
    half = head_dim // 2
    inv_freq = ROPE_THETA ** (-jnp.arange(half, dtype=F32) / half)
    ang = pos.astype(F32)[:, None] * inv_freq[None, :]
    cos, sin = jnp.cos(ang), jnp.sin(ang)
    return jnp.concatenate([cos, cos], axis=1), jnp.concatenate([-sin, sin], axis=1)


def _layer(x, mod, pos_rows, group, lw, n_heads, head_dim, d_conv):
    (norm1_g, w_in, conv_w, conv_b, ln_g, ln_b, w_ap, w_cp, w_out, norm2_g, w_ff1, w_ff2) = lw
    m, d = x.shape
    d_attn = n_heads * head_dim
    off_k, off_v, off_u = d_attn, 2 * d_attn, 3 * d_attn
    off_ug, off_ga = off_u + d_conv, off_u + 2 * d_conv
    off_gc = off_ga + d
    tabs = _rope_tables(pos_rows, head_dim)

    h = _norm_mod(x, norm1_g, mod, 1, 0)
    glu = _glu_proj(h, w_in, off_u, off_ug, d_conv)
    if group[0] == "prompt":
        _, batch, seq = group
        (q_b,) = _proj(h, w_in, 0, d_attn, [BF16], seq, tabs, head_dim)
        k_f, k_b = _proj(h, w_in, off_k, d_attn, [F32, BF16], seq, tabs, head_dim)
        v_f, v_b = _proj(h, w_in, off_v, d_attn, [F32, BF16], seq)
        attn = _moba_prompt(q_b, k_b, v_b, batch, seq, n_heads, head_dim)
        glu3 = glu.reshape(batch, seq, d_conv)
        cv = _conv_module(glu3, None, conv_w, conv_b, ln_g, ln_b, BF16).reshape(m, d_conv)
        conv_state = glu3[:, seq - (conv_w.shape[0] - 1):]
    else:
        _, cache_k3, cache_v3, state, page_table = group
        bs, n_pages = page_table.shape
        tq = m // bs
        (q_f,) = _proj(h, w_in, 0, d_attn, [F32], m, tabs, head_dim)
        (k_f,) = _proj(h, w_in, off_k, d_attn, [F32], m, tabs, head_dim)
        (v_f,) = _proj(h, w_in, off_v, d_attn, [F32], m)
        tok_rows = lambda a: a.reshape(bs, tq * n_heads, head_dim)
        attn = _moba_sample(tok_rows(q_f), tok_rows(k_f), tok_rows(v_f), cache_k3, cache_v3,
                            page_table.reshape(-1), n_pages, n_heads, head_dim)
        attn = attn.reshape(m, d_attn).astype(BF16)
        glu3 = glu.reshape(bs, tq, d_conv)
        cv = _conv_module(glu3, state, conv_w, conv_b, ln_g, ln_b, F32).reshape(m, d_conv).astype(BF16)
        conv_state = jnp.concatenate([state, glu3], axis=1)[:, tq:]

    mixed = _merge(h, attn, cv, w_in, off_ga, off_gc, w_ap, w_cp)
    x1 = _resid_proj(mixed, w_out, x, mod, 2)
    h2 = _norm_mod(x1, norm2_g, mod, 4, 3)
    x2 = _ff2(_ff1(h2, w_ff1), w_ff2, x1, mod, 5)
    return x2, k_f, v_f, conv_state


def kernel(x_prompt, x_sample, cache_k, cache_v, state_conv, page_table, c_prompt, c_sample, ada_w, ada_b,
           norm1_g, w_in, conv_w, conv_b, conv_ln_g, conv_ln_b, w_attn_proj, w_conv_proj, w_out, norm2_g,
           w_ff1, w_ff2, final_g):
    bp, seq, d = x_prompt.shape
    bs, tq, _ = x_sample.shape
    depth, n_pool, page, n_heads, head_dim = cache_k.shape
    d_conv = state_conv.shape[-1]
    n_pages = page_table.shape[1]
    assert MOBA_BLOCK % page == 0 and (n_pages * page) % MOBA_BLOCK == 0 and tq <= MOBA_BLOCK

    yp = x_prompt.reshape(bp * seq, d)
    ys = x_sample.reshape(bs * tq, d)
    n_c = bp + bs
    c_rows = -(-n_c // 8) * 8
    c_all = jnp.concatenate([c_prompt, c_sample, jnp.zeros((c_rows - n_c, d), F32)], axis=0)
    pos_p = jnp.arange(seq, dtype=jnp.int32)
    pos_s = n_pages * page + jnp.tile(jnp.arange(tq, dtype=jnp.int32), bs)

    outs = [[] for _ in range(6)]
    for l in range(depth):
        mod_all = _ada(c_all, ada_w[l], ada_b[l])
        mod_p = _Mod(mod_all[:bp].reshape(bp * 6, 1, d), False, d, seq)
        mod_s = _Mod(jnp.repeat(mod_all[bp:n_c], tq, axis=0), True, d, bs * tq)
        lw = (norm1_g[l], w_in[l].astype(BF16), conv_w[l], conv_b[l], conv_ln_g[l], conv_ln_b[l],
              w_attn_proj[l].astype(BF16), w_conv_proj[l].astype(BF16), w_out[l].astype(BF16), norm2_g[l],
              w_ff1[l].astype(BF16), w_ff2[l].astype(BF16))
        yp, kp, vp, cp = _layer(yp, mod_p, pos_p, ("prompt", bp, seq), lw, n_heads, head_dim, d_conv)
        group_s = ("sample", cache_k[l].reshape(n_pool, page * n_heads, head_dim),
                   cache_v[l].reshape(n_pool, page * n_heads, head_dim), state_conv[l], page_table)
        ys, kn, vn, cn = _layer(ys, mod_s, pos_s, group_s, lw, n_heads, head_dim, d_conv)
        for lst, val in zip(outs, (kp.reshape(bp, seq, n_heads, head_dim), vp.reshape(bp, seq, n_heads, head_dim), cp,
                                   kn.reshape(bs, tq, n_heads, head_dim), vn.reshape(bs, tq, n_heads, head_dim), cn)):
            lst.append(val)
    y_prompt = _final_norm(yp, final_g).reshape(bp, seq, d)
    y_sample = _final_norm(ys, final_g).reshape(bs, tq, d)
    return (y_prompt, y_sample) + tuple(jnp.stack(o) for o in outs)
```

```python
import functools

import jax
import jax.numpy as jnp
from jax import lax
from jax.experimental import pallas as pl
from jax.experimental.pallas import tpu as pltpu

MOBA_BLOCK = 256
MOBA_TOPK = 3
ROPE_THETA = 10000.0
EPS = 1e-6
LANES = 128
SUBLANES = 8
HALO = 32
VMEM_LIMIT = 56 * 1024 * 1024

F32 = jnp.float32
BF16 = jnp.bfloat16


def _pick(n, pref):
    if n <= pref:
        return n
    t = pref
    while n % t:
        t //= 2
    return t


def _params(n_axes):
    return pltpu.CompilerParams(dimension_semantics=("arbitrary",) * n_axes,
                                vmem_limit_bytes=VMEM_LIMIT)


def _dot(a, b):
    return jnp.dot(a, b, preferred_element_type=F32)


def _dot_nt(a, b):
    return lax.dot_general(a, b, (((1,), (1,)), ((), ())), preferred_element_type=F32)


def _row(ref):
    v = ref[...]
    return v.reshape(v.shape[-2:])


class _Mod:
    def __init__(self, arr, per_row, d_model, rows_per_batch):
        self.arr, self.per_row, self.d, self.rpb = arr, per_row, d_model, rows_per_batch

    def spec(self, which, tm, tn):
        if self.per_row:
            nj = self.d // tn
            return pl.BlockSpec((tm, tn), lambda i, j, *_: (i, which * nj + j))
        tpb = self.rpb // tm
        return pl.BlockSpec((1, 1, tn), lambda i, j, *_: ((i // tpb) * 6 + which, 0, j))


def _ada_kernel(c_ref, w_ref, b_ref, o_ref):
    o_ref[...] = _dot(c_ref[...].astype(BF16), w_ref[...].astype(BF16)) + b_ref[...]


def _ada(c, w, b):
    r, d = c.shape
    n = w.shape[1]
    tn = _pick(n, 512)
    return pl.pallas_call(
        _ada_kernel, grid=(n // tn,),
        in_specs=[pl.BlockSpec((r, d), lambda j: (0, 0)),
                  pl.BlockSpec((d, tn), lambda j: (0, j)),
                  pl.BlockSpec((1, tn), lambda j: (0, j))],
        out_specs=pl.BlockSpec((r, tn), lambda j: (0, j)),
        out_shape=jax.ShapeDtypeStruct((r, n), F32),
        compiler_params=_params(1), name="ada_mod")(c, w, b.reshape(1, n))


def _norm_mod_kernel(x_ref, g_ref, sc_ref, sh_ref, o_ref):
    x = x_ref[...]
    y = x * lax.rsqrt(jnp.mean(x * x, axis=-1, keepdims=True) + EPS) * g_ref[...]
    o_ref[...] = (y * (1.0 + _row(sc_ref)) + _row(sh_ref)).astype(o_ref.dtype)


def _norm_mod(x, g, mod, which_sc, which_sh):
    m, d = x.shape
    tm = _pick(mod.rpb, 512)
    return pl.pallas_call(
        _norm_mod_kernel, grid=(m // tm, 1),
        in_specs=[pl.BlockSpec((tm, d), lambda i, j: (i, 0)),
                  pl.BlockSpec((1, d), lambda i, j: (0, 0)),
                  mod.spec(which_sc, tm, d), mod.spec(which_sh, tm, d)],
        out_specs=pl.BlockSpec((tm, d), lambda i, j: (i, 0)),
        out_shape=jax.ShapeDtypeStruct((m, d), BF16),
        compiler_params=_params(2), name="norm_mod")(x, g.reshape(1, d), mod.arr, mod.arr)


def _final_norm_kernel(x_ref, g_ref, o_ref):
    x = x_ref[...]
    o_ref[...] = x * lax.rsqrt(jnp.mean(x * x, axis=-1, keepdims=True) + EPS) * g_ref[...]


def _final_norm(x, g):
    m, d = x.shape
    tm = _pick(m, 512)
    return pl.pallas_call(
        _final_norm_kernel, grid=(m // tm,),
        in_specs=[pl.BlockSpec((tm, d), lambda i: (i, 0)), pl.BlockSpec((1, d), lambda i: (0, 0))],
        out_specs=pl.BlockSpec((tm, d), lambda i: (i, 0)),
        out_shape=jax.ShapeDtypeStruct((m, d), F32),
        compiler_params=_params(1), name="final_norm")(x, g.reshape(1, d))


def _proj_kernel(*refs, rope, head_dim):
    a_ref, w_ref = refs[:2]
    outs = refs[4:] if rope else refs[2:]
    acc = _dot(a_ref[...], w_ref[...])
    if not rope:
        for o in outs:
            o[...] = acc.astype(o.dtype)
        return
    cos, sin = refs[2][...], refs[3][...]
    for c in range(acc.shape[1] // head_dim):
        xh = acc[:, c * head_dim:(c + 1) * head_dim]
        r = xh * cos + pltpu.roll(xh, head_dim // 2, 1) * sin
        for o in outs:
            o[:, c * head_dim:(c + 1) * head_dim] = r.astype(o.dtype)


def _proj(a, w, col_off, n, out_dtypes, rpb, rope_tabs=None, head_dim=LANES):
    m, k = a.shape
    tm, tn = _pick(rpb, 1024), _pick(n, 1024)
    in_specs = [pl.BlockSpec((tm, k), lambda i, j: (i, 0)),
                pl.BlockSpec((k, tn), lambda i, j: (0, col_off // tn + j))]
    args = [a, w]
    if rope_tabs is not None:
        tpb = rope_tabs[0].shape[0] // tm
        tab_spec = pl.BlockSpec((tm, head_dim), lambda i, j: (i % tpb, 0))
        in_specs += [tab_spec, tab_spec]
        args += list(rope_tabs)
    return pl.pallas_call(
        functools.partial(_proj_kernel, rope=rope_tabs is not None, head_dim=head_dim),
        grid=(m // tm, n // tn), in_specs=in_specs,
        out_specs=[pl.BlockSpec((tm, tn), lambda i, j: (i, j)) for _ in out_dtypes],
        out_shape=[jax.ShapeDtypeStruct((m, n), dt) for dt in out_dtypes],
        compiler_params=_params(2), name="in_proj")(*args)


def _glu_kernel(a_ref, wu_ref, wg_ref, o_ref):
    a = a_ref[...]
    o_ref[...] = _dot(a, wu_ref[...]) * jax.nn.sigmoid(_dot(a, wg_ref[...]))


def _glu_proj(a, w, off_u, off_g, n):
    m, k = a.shape
    tm, tn = _pick(m, 1024), _pick(n, 512)
    return pl.pallas_call(
        _glu_kernel, grid=(m // tm, n // tn),
        in_specs=[pl.BlockSpec((tm, k), lambda i, j: (i, 0)),
                  pl.BlockSpec((k, tn), lambda i, j: (0, off_u // tn + j)),
                  pl.BlockSpec((k, tn), lambda i, j: (0, off_g // tn + j))],
        out_specs=pl.BlockSpec((tm, tn), lambda i, j: (i, j)),
        out_shape=jax.ShapeDtypeStruct((m, n), F32),
        compiler_params=_params(2), name="glu_proj")(a, w, w)


def _conv_kernel(prev_ref, cur_ref, w_ref, b_ref, g_ref, beta_ref, o_ref, pad_ref, conv_ref, *shift_refs,
                 zero_first, taps):
    tt, ch = cur_ref.shape[1], cur_ref.shape[2]
    prev_rows = prev_ref.shape[1]
    prev = prev_ref[0]
    if zero_first:
        prev = jnp.where(pl.program_id(1) > 0, prev, 0.0)
    pad_ref[HALO - prev_rows:HALO, :] = prev
    pad_ref[HALO:HALO + tt, :] = cur_ref[0]
    first = HALO - (taps - 1)
    rc, cc = min(32, tt), min(256, ch)
    if shift_refs:
        (shift_ref,) = shift_refs
        n_sh = shift_ref.shape[1]
        for s in range(1, SUBLANES):
            for c0 in range(0, ch, cc):
                shift_ref[s - 1, :, c0:c0 + cc] = pad_ref[s:s + n_sh, c0:c0 + cc]

    def tap_rows(r0, t, c0):
        a, s = divmod(first + t, SUBLANES)
        if not shift_refs:
            return pad_ref[first + r0 + t:first + r0 + t + rc, c0:c0 + cc]
        src = pad_ref if s == 0 else shift_refs[0].at[s - 1]
        return src[r0 + a * SUBLANES:r0 + a * SUBLANES + rc, c0:c0 + cc]

    for r0 in range(0, tt, rc):
        for c0 in range(0, ch, cc):
            acc = jnp.broadcast_to(b_ref[:, c0:c0 + cc], (rc, cc))
            for t in range(taps):
                acc = acc + tap_rows(r0, t, c0) * w_ref[t:t + 1, c0:c0 + cc]
            conv_ref[r0:r0 + rc, c0:c0 + cc] = acc
    lr = min(16, tt)
    g, beta = g_ref[...], beta_ref[...]
    for r0 in range(0, tt, lr):
        y = conv_ref[r0:r0 + lr, :]
        d = y - jnp.mean(y, axis=-1, keepdims=True)
        z = d * lax.rsqrt(jnp.mean(d * d, axis=-1, keepdims=True) + EPS) * g + beta
        o_ref[0, r0:r0 + lr, :] = (z * jax.nn.sigmoid(z)).astype(o_ref.dtype)


def _conv_module(glu, prev, conv_w, conv_b, ln_g, ln_b, out_dtype):
    b, t, ch = glu.shape
    taps = conv_w.shape[0]
    if prev is None:
        tt = _pick(t, 128)
        hb = tt // HALO
        prev_arr = glu
        prev_spec = pl.BlockSpec((1, HALO, ch), lambda bi, i: (bi, jnp.maximum(i * hb - 1, 0), 0))
    else:
        tt = t
        prev_arr = prev
        prev_spec = pl.BlockSpec((1, taps - 1, ch), lambda bi, i: (bi, 0, 0))
    vec = pl.BlockSpec((1, ch), lambda bi, i: (0, 0))
    scratch = [pltpu.VMEM((HALO + tt, ch), F32), pltpu.VMEM((tt, ch), F32)]
    if prev is None and tt % SUBLANES == 0:
        scratch.append(pltpu.VMEM((SUBLANES - 1, HALO + tt - SUBLANES, ch), F32))
    return pl.pallas_call(
        functools.partial(_conv_kernel, zero_first=prev is None, taps=taps),
        grid=(b, t // tt),
        in_specs=[prev_spec, pl.BlockSpec((1, tt, ch), lambda bi, i: (bi, i, 0)),
                  pl.BlockSpec((taps, ch), lambda bi, i: (0, 0)), vec, vec, vec],
        out_specs=pl.BlockSpec((1, tt, ch), lambda bi, i: (bi, i, 0)),
        out_shape=jax.ShapeDtypeStruct((b, t, ch), out_dtype),
        scratch_shapes=scratch,
        compiler_params=_params(2), name="conv_module")(
            prev_arr, glu, conv_w, conv_b.reshape(1, ch), ln_g.reshape(1, ch), ln_b.reshape(1, ch))


def _moba_prompt_kernel(q_ref, k_ref, v_ref, o_ref, kmean_ref, *, n_blocks, scale):
    blk = MOBA_BLOCK
    kmean_ref[...] = jnp.zeros_like(kmean_ref)
    for n in range(n_blocks):
        kb = k_ref[n * blk:(n + 1) * blk, :].astype(F32)
        kmean_ref[n:n + 1, :] = jnp.sum(kb, axis=0, keepdims=True) * (1.0 / blk)
    km = kmean_ref[...]
    km_hi = km.astype(BF16)
    km_lo = (km - km_hi.astype(F32)).astype(BF16)
    row = lax.broadcasted_iota(jnp.int32, (blk, blk), 0)
    col = lax.broadcasted_iota(jnp.int32, (blk, blk), 1)
    lane = lax.broadcasted_iota(jnp.int32, (blk, LANES), 1)

    def attend(own):
        q = q_ref[own * blk:(own + 1) * blk, :]
        keep = []
        if own > MOBA_TOPK:
            gate = _dot_nt(q, km_hi) + _dot_nt(q, km_lo)
            rank = jnp.zeros(gate.shape, jnp.int32)
            for m in range(own):
                gm = gate[:, m:m + 1]
                beats = (gm > gate) | ((gm == gate) & (m < lane))
                rank = rank + jnp.where(beats, 1, 0)
            keep = [rank[:, j:j + 1] < MOBA_TOPK for j in range(own)]
        scores = []
        for j in range(own + 1):
            s = _dot_nt(q, k_ref[j * blk:(j + 1) * blk, :]) * scale
            if j == own:
                s = jnp.where(col <= row, s, -jnp.inf)
            elif keep:
                s = jnp.where(keep[j], s, -jnp.inf)
            scores.append(s)
        m_i = jnp.max(scores[0], axis=1, keepdims=True)
        for s in scores[1:]:
            m_i = jnp.maximum(m_i, jnp.max(s, axis=1, keepdims=True))
        probs = [jnp.exp(s - m_i) for s in scores]
        l_i = jnp.sum(probs[0], axis=1, keepdims=True)
        for p in probs[1:]:
            l_i = l_i + jnp.sum(p, axis=1, keepdims=True)
        p_all = jnp.concatenate([p.astype(BF16) for p in probs], axis=1)
        acc = _dot(p_all, v_ref[0:(own + 1) * blk, :])
        o_ref[own * blk:(own + 1) * blk, :] = (acc / l_i).astype(o_ref.dtype)

    for own in range(n_blocks):
        attend(own)


def _moba_prompt(q, k, v, batch, seq, n_heads, head_dim):
    assert seq % MOBA_BLOCK == 0 and seq // MOBA_BLOCK <= LANES
    spec = pl.BlockSpec((seq, head_dim), lambda b, h: (b, h))
    return pl.pallas_call(
        functools.partial(_moba_prompt_kernel, n_blocks=seq // MOBA_BLOCK, scale=head_dim ** -0.5),
        grid=(batch, n_heads),
        in_specs=[spec, spec, spec], out_specs=spec,
        out_shape=jax.ShapeDtypeStruct(q.shape, BF16),
        scratch_shapes=[pltpu.VMEM((LANES, head_dim), F32)],
        compiler_params=_params(2), name="moba_prompt")(q, k, v)


def _moba_sample_kernel(pt_ref, q_ref, kn_ref, vn_ref, *rest, n_heads, tq, blocks_per_step, ppb, nbp, scale):
    n_k = blocks_per_step * ppb
    k_refs, v_refs, o_ref = rest[:n_k], rest[n_k:2 * n_k], rest[2 * n_k]
    acc_s, gate_s, m_s, l_s = rest[2 * n_k + 1:]
    g = pl.program_id(1)
    rows, hd = q_ref.shape[1], q_ref.shape[2]
    page = k_refs[0].shape[1] // n_heads
    lane = lax.broadcasted_iota(jnp.int32, (rows, LANES), 1)
    row = lax.broadcasted_iota(jnp.int32, (rows, LANES), 0)
    same_head = (lane % n_heads) == (row % n_heads)
    chunks = page * n_heads // LANES
    q = q_ref[0]
    q_bf = q.astype(BF16)

    @pl.when(g == 0)
    def _():
        gate_s[...] = jnp.zeros_like(gate_s)
        m_s[...] = jnp.zeros_like(m_s)
        l_s[...] = jnp.zeros_like(l_s)

    for j in range(blocks_per_step):
        n = g * blocks_per_step + j
        hit = lane == n
        ksum = jnp.zeros((n_heads, hd), F32)
        scores = []
        for i in range(ppb):
            kp = k_refs[j * ppb + i][0]
            ksum = ksum + jnp.sum(kp.reshape(page, n_heads, hd), axis=0)
            s = _dot_nt(q_bf, kp.astype(BF16)) * scale
            scores += [jnp.where(same_head, s[:, c * LANES:(c + 1) * LANES], -jnp.inf) for c in range(chunks)]
        kmean = ksum * (1.0 / MOBA_BLOCK)
        gate = jnp.sum(q * jnp.concatenate([kmean] * tq, axis=0), axis=1, keepdims=True)
        m_c = scores[0]
        for s in scores[1:]:
            m_c = jnp.maximum(m_c, s)
        m_b = jnp.max(m_c, axis=1, keepdims=True)
        probs = [jnp.exp(s - m_b) for s in scores]
        l_c = probs[0]
        for p in probs[1:]:
            l_c = l_c + p
        acc = jnp.zeros((rows, hd), F32)
        for i in range(ppb):
            p_i = jnp.concatenate([p.astype(BF16) for p in probs[i * chunks:(i + 1) * chunks]], axis=1)
            acc = acc + _dot(p_i, v_refs[j * ppb + i][0].astype(BF16))
        acc_s[n] = acc
        gate_s[...] = jnp.where(hit, gate, gate_s[...])
        m_s[...] = jnp.where(hit, m_b, m_s[...])
        l_s[...] = jnp.where(hit, jnp.sum(l_c, axis=1, keepdims=True), l_s[...])

    @pl.when(g == pl.num_programs(1) - 1)
    def _():
        t_id = lax.broadcasted_iota(jnp.int32, (rows, 1), 0) // n_heads
        kn, vn = kn_ref[0], vn_ref[0]
        kn_c = [jnp.concatenate([kn[c * n_heads:(c + 1) * n_heads]] * tq, axis=0) for c in range(tq)]
        vn_c = [jnp.concatenate([vn[c * n_heads:(c + 1) * n_heads]] * tq, axis=0) for c in range(tq)]
        s_own = [jnp.where(t_id >= c, jnp.sum(q * kn_c[c], axis=1, keepdims=True) * scale, -jnp.inf)
                 for c in range(tq)]
        gt, ms, ls = gate_s[...], m_s[...], l_s[...]
        rank = jnp.zeros(gt.shape, jnp.int32)
        for m in range(nbp):
            gm = gt[:, m:m + 1]
            beats = (gm > gt) | ((gm == gt) & (m < lane))
            rank = rank + jnp.where(beats, 1, 0)
        sel = (lane < nbp) & (rank < MOBA_TOPK)
        m_all = jnp.max(jnp.where(sel, ms, -jnp.inf), axis=1, keepdims=True)
        for s in s_own:
            m_all = jnp.maximum(m_all, s)
        w = jnp.where(sel, jnp.exp(ms - m_all), 0.0)
        den = jnp.sum(w * ls, axis=1, keepdims=True)
        num = jnp.zeros((rows, hd), F32)
        for c, s in enumerate(s_own):
            p = jnp.exp(s - m_all)
            den = den + p
            num = num + p * vn_c[c]
        for n in range(nbp):
            num = num + w[:, n:n + 1] * acc_s[n]
        o_ref[0] = num / den


def _moba_sample(q3, kn3, vn3, cache_k3, cache_v3, pt_flat, n_pages, n_heads, head_dim):
    bs, rows, _ = q3.shape
    tq = rows // n_heads
    page = cache_k3.shape[1] // n_heads
    ppb = MOBA_BLOCK // page
    nbp = n_pages // ppb
    assert MOBA_TOPK <= nbp <= LANES and LANES % n_heads == 0 and rows % 8 == 0
    blocks_per_step = next(c for c in (4, 2, 1) if nbp % c == 0)
    n_k = blocks_per_step * ppb
    tok = pl.BlockSpec((1, rows, head_dim), lambda b, g, pt: (b, 0, 0))
    pages = [pl.BlockSpec((1, page * n_heads, head_dim),
                          lambda b, g, pt, i=i: (pt[b * n_pages + g * n_k + i], 0, 0)) for i in range(n_k)]
    stat = pltpu.VMEM((rows, LANES), F32)
    return pl.pallas_call(
        functools.partial(_moba_sample_kernel, n_heads=n_heads, tq=tq, blocks_per_step=blocks_per_step,
                          ppb=ppb, nbp=nbp, scale=head_dim ** -0.5),
        grid_spec=pltpu.PrefetchScalarGridSpec(
            num_scalar_prefetch=1, grid=(bs, nbp // blocks_per_step),
            in_specs=[tok, tok, tok] + pages + pages, out_specs=tok,
            scratch_shapes=[pltpu.VMEM((nbp, rows, head_dim), F32), stat, stat, stat]),
        out_shape=jax.ShapeDtypeStruct((bs, rows, head_dim), F32),
        compiler_params=_params(2), name="moba_sample")(
            pt_flat, q3, kn3, vn3, *([cache_k3] * n_k), *([cache_v3] * n_k))


def _merge_kernel(h_ref, at_ref, cv_ref, wga_ref, wgc_ref, wap_ref, wcp_ref, o_ref):
    h = h_ref[...]
    ga = jax.nn.sigmoid(_dot(h, wga_ref[...]))
    gc = jax.nn.sigmoid(_dot(h, wgc_ref[...]))
    o_ref[...] = (ga * _dot(at_ref[...], wap_ref[...]) + gc * _dot(cv_ref[...], wcp_ref[...])).astype(o_ref.dtype)


def _merge(h, attn, cv, w_in, off_ga, off_gc, w_ap, w_cp):
    m, d = h.shape
    n = w_ap.shape[1]
    tm, tn = _pick(m, 512), _pick(n, 512)
    return pl.pallas_call(
        _merge_kernel, grid=(m // tm, n // tn),
        in_specs=[pl.BlockSpec((tm, d), lambda i, j: (i, 0)),
                  pl.BlockSpec((tm, attn.shape[1]), lambda i, j: (i, 0)),
                  pl.BlockSpec((tm, cv.shape[1]), lambda i, j: (i, 0)),
                  pl.BlockSpec((d, tn), lambda i, j: (0, off_ga // tn + j)),
                  pl.BlockSpec((d, tn), lambda i, j: (0, off_gc // tn + j)),
                  pl.BlockSpec((w_ap.shape[0], tn), lambda i, j: (0, j)),
                  pl.BlockSpec((w_cp.shape[0], tn), lambda i, j: (0, j))],
        out_specs=pl.BlockSpec((tm, tn), lambda i, j: (i, j)),
        out_shape=jax.ShapeDtypeStruct((m, n), BF16),
        compiler_params=_params(2), name="gated_merge")(h, attn, cv, w_in, w_in, w_ap, w_cp)


def _resid_kernel(a_ref, w_ref, x_ref, g_ref, o_ref):
    o_ref[...] = x_ref[...] + _row(g_ref) * _dot(a_ref[...], w_ref[...])


def _resid_proj(a, w, x, mod, which):
    m, k = a.shape
    n = w.shape[1]
    tm, tn = _pick(mod.rpb, 1024), _pick(n, 512)
    return pl.pallas_call(
        _resid_kernel, grid=(m // tm, n // tn),
        in_specs=[pl.BlockSpec((tm, k), lambda i, j: (i, 0)),
                  pl.BlockSpec((k, tn), lambda i, j: (0, j)),
                  pl.BlockSpec((tm, tn), lambda i, j: (i, j)),
                  mod.spec(which, tm, tn)],
        out_specs=pl.BlockSpec((tm, tn), lambda i, j: (i, j)),
        out_shape=jax.ShapeDtypeStruct((m, n), F32),
        compiler_params=_params(2), name="out_proj")(a, w, x, mod.arr)


def _ff1_kernel(a_ref, w_ref, o_ref):
    r = jnp.maximum(_dot(a_ref[...], w_ref[...]), 0.0)
    o_ref[...] = (r * r).astype(o_ref.dtype)


def _ff1(a, w):
    m, k = a.shape
    n = w.shape[1]
    tm, tn = _pick(m, 1024), _pick(n, 1024)
    return pl.pallas_call(
        _ff1_kernel, grid=(m // tm, n // tn),
        in_specs=[pl.BlockSpec((tm, k), lambda i, j: (i, 0)), pl.BlockSpec((k, tn), lambda i, j: (0, j))],
        out_specs=pl.BlockSpec((tm, tn), lambda i, j: (i, j)),
        out_shape=jax.ShapeDtypeStruct((m, n), BF16),
        compiler_params=_params(2), name="ff1")(a, w)


def _ff2_kernel(a_ref, w_ref, x_ref, g_ref, o_ref, acc_ref):
    kk = pl.program_id(2)

    @pl.when(kk == 0)
    def _():
        acc_ref[...] = jnp.zeros_like(acc_ref)

    acc_ref[...] += _dot(a_ref[...], w_ref[...])

    @pl.when(kk == pl.num_programs(2) - 1)
    def _():
        o_ref[...] = x_ref[...] + _row(g_ref) * acc_ref[...]


def _ff2(a, w, x, mod, which):
    m, k = a.shape
    n = w.shape[1]
    tm, tn, tk = _pick(mod.rpb, 1024), _pick(n, 1024), _pick(k, 2048)
    return pl.pallas_call(
        _ff2_kernel, grid=(m // tm, n // tn, k // tk),
        in_specs=[pl.BlockSpec((tm, tk), lambda i, j, kk: (i, kk)),
                  pl.BlockSpec((tk, tn), lambda i, j, kk: (kk, j)),
                  pl.BlockSpec((tm, tn), lambda i, j, kk: (i, j)),
                  mod.spec(which, tm, tn)],
        out_specs=pl.BlockSpec((tm, tn), lambda i, j, kk: (i, j)),
        out_shape=jax.ShapeDtypeStruct((m, n), F32),
        scratch_shapes=[pltpu.VMEM((tm, tn), F32)],
        compiler_params=_params(3), name="ff2")(a, w, x, mod.arr)


def _rope_tables(pos, head_dim):
    half = head_dim // 2
    inv_freq = ROPE_THETA ** (-jnp.arange(half, dtype=F32) / half)
    ang = pos.astype(F32)[:, None] * inv_freq[None, :]
    cos, sin = jnp.cos(ang), jnp.sin(ang)
    return jnp.concatenate([cos, cos], axis=1), jnp.concatenate([-sin, sin], axis=1)


def _layer(x, mod, pos_rows, group, lw, n_heads, head_dim, d_conv):
    (norm1_g, w_in, conv_w, conv_b, ln_g, ln_b, w_ap, w_cp, w_out, norm2_g, w_ff1, w_ff2) = lw
    m, d = x.shape
    d_attn = n_heads * head_dim
    off_k, off_v, off_u = d_attn, 2 * d_attn, 3 * d_attn
    off_ug, off_ga = off_u + d_conv, off_u + 2 * d_conv
    off_gc = off_ga + d
    tabs = _rope_tables(pos_rows, head_dim)

    h = _norm_mod(x, norm1_g, mod, 1, 0)
    glu = _glu_proj(h, w_in, off_u, off_ug, d_conv)
    if group[0] == "prompt":
        _, batch, seq = group
        (q_b,) = _proj(h, w_in, 0, d_attn, [BF16], seq, tabs, head_dim)
        k_f, k_b = _proj(h, w_in, off_k, d_attn, [F32, BF16], seq, tabs, head_dim)
        v_f, v_b = _proj(h, w_in, off_v, d_attn, [F32, BF16], seq)
        attn = _moba_prompt(q_b, k_b, v_b, batch, seq, n_heads, head_dim)
        glu3 = glu.reshape(batch, seq, d_conv)
        cv = _conv_module(glu3, None, conv_w, conv_b, ln_g, ln_b, BF16).reshape(m, d_conv)
        conv_state = glu3[:, seq - (conv_w.shape[0] - 1):]
    else:
        _, cache_k3, cache_v3, state, page_table = group
        bs, n_pages = page_table.shape
        tq = m // bs
        (q_f,) = _proj(h, w_in, 0, d_attn, [F32], m, tabs, head_dim)
        (k_f,) = _proj(h, w_in, off_k, d_attn, [F32], m, tabs, head_dim)
        (v_f,) = _proj(h, w_in, off_v, d_attn, [F32], m)
        tok_rows = lambda a: a.reshape(bs, tq * n_heads, head_dim)
        attn = _moba_sample(tok_rows(q_f), tok_rows(k_f), tok_rows(v_f), cache_k3, cache_v3,
                            page_table.reshape(-1), n_pages, n_heads, head_dim)
        attn = attn.reshape(m, d_attn).astype(BF16)
        glu3 = glu.reshape(bs, tq, d_conv)
        cv = _conv_module(glu3, state, conv_w, conv_b, ln_g, ln_b, F32).reshape(m, d_conv).astype(BF16)
        conv_state = jnp.concatenate([state, glu3], axis=1)[:, tq:]

    mixed = _merge(h, attn, cv, w_in, off_ga, off_gc, w_ap, w_cp)
    x1 = _resid_proj(mixed, w_out, x, mod, 2)
    h2 = _norm_mod(x1, norm2_g, mod, 4, 3)
    x2 = _ff2(_ff1(h2, w_ff1), w_ff2, x1, mod, 5)
    return x2, k_f, v_f, conv_state


def kernel(x_prompt, x_sample, cache_k, cache_v, state_conv, page_table, c_prompt, c_sample, ada_w, ada_b,
           norm1_g, w_in, conv_w, conv_b, conv_ln_g, conv_ln_b, w_attn_proj, w_conv_proj, w_out, norm2_g,
           w_ff1, w_ff2, final_g):
    bp, seq, d = x_prompt.shape
    bs, tq, _ = x_sample.shape
    depth, n_pool, page, n_heads, head_dim = cache_k.shape
    d_conv = state_conv.shape[-1]
    n_pages = page_table.shape[1]
    assert MOBA_BLOCK % page == 0 and (n_pages * page) % MOBA_BLOCK == 0 and tq <= MOBA_BLOCK

    yp = x_prompt.reshape(bp * seq, d)
    ys = x_sample.reshape(bs * tq, d)
    n_c = bp + bs
    c_rows = -(-n_c // 8) * 8
    c_all = jnp.concatenate([c_prompt, c_sample, jnp.zeros((c_rows - n_c, d), F32)], axis=0)
    pos_p = jnp.arange(seq, dtype=jnp.int32)
    pos_s = n_pages * page + jnp.tile(jnp.arange(tq, dtype=jnp.int32), bs)

    outs = [[] for _ in range(6)]
    for l in range(depth):
        mod_all = _ada(c_all, ada_w[l], ada_b[l])
        mod_p = _Mod(mod_all[:bp].reshape(bp * 6, 1, d), False, d, seq)
        mod_s = _Mod(jnp.repeat(mod_all[bp:n_c], tq, axis=0), True, d, bs * tq)
        lw = (norm1_g[l], w_in[l].astype(BF16), conv_w[l], conv_b[l], conv_ln_g[l], conv_ln_b[l],
              w_attn_proj[l].astype(BF16), w_conv_proj[l].astype(BF16), w_out[l].astype(BF16), norm2_g[l],
              w_ff1[l].astype(BF16), w_ff2[l].astype(BF16))
        yp, kp, vp, cp = _layer(yp, mod_p, pos_p, ("prompt", bp, seq), lw, n_heads, head_dim, d_conv)
        group_s = ("sample", cache_k[l].reshape(n_pool, page * n_heads, head_dim),
                   cache_v[l].reshape(n_pool, page * n_heads, head_dim), state_conv[l], page_table)
        ys, kn, vn, cn = _layer(ys, mod_s, pos_s, group_s, lw, n_heads, head_dim, d_conv)
        for lst, val in zip(outs, (kp.reshape(bp, seq, n_heads, head_dim), vp.reshape(bp, seq, n_heads, head_dim), cp,
                                   kn.reshape(bs, tq, n_heads, head_dim), vn.reshape(bs, tq, n_heads, head_dim), cn)):
            lst.append(val)
    y_prompt = _final_norm(yp, final_g).reshape(bp, seq, d)
    y_sample = _final_norm(ys, final_g).reshape(bs, tq, d)
    return (y_prompt, y_sample) + tuple(jnp.stack(o) for o in outs)
```
